```python
import math
import jax, jax.numpy as jnp
from jax import lax
import numpy as np

D_MODEL = 1024
BATCH = 16
SEQ = 2048
DEPTH = 2

CTX_LEN = 256
GRID_W = 64
N_GROUPS = 4
GROUP_W = D_MODEL // N_GROUPS
N_SPLITS = 10
D_IN = N_SPLITS * GROUP_W
N_HEADS_A = 4
HEAD_A = GROUP_W // N_HEADS_A
CHUNK = 128
N_HEADS_B = 4
HEAD_V_B = GROUP_W // N_HEADS_B
HEAD_QK_B = HEAD_V_B // 2
Q_BLOCK = 128
ROPE_BASE = 10000.0
CONV_C = 31
CONV_D = 3
N_EXPERTS = 32
TOP_K = 4
D_FF = D_MODEL
SWIGLU_LIMIT = 7.0
SWIGLU_ALPHA = 1.702
EPS = 1e-6

kernel_name = "hybrid_headgroup_diffusion_moe"


def rms_norm(x, g):
    xf = x.astype(jnp.float32)
    y = xf * lax.rsqrt(jnp.mean(xf * xf, axis=-1, keepdims=True) + EPS)
    return (y * g.astype(jnp.float32)).astype(x.dtype)


def layer_norm(x, g, b):
    xf = x.astype(jnp.float32)
    mu = jnp.mean(xf, axis=-1, keepdims=True)
    var = jnp.mean(jnp.square(xf - mu), axis=-1, keepdims=True)
    y = (xf - mu) * lax.rsqrt(var + EPS) * g.astype(jnp.float32) + b.astype(jnp.float32)
    return y.astype(x.dtype)


def axial_rope_tables(n):
    rows = n // GRID_W
    row = jnp.repeat(jnp.arange(rows, dtype=jnp.float32), GRID_W)
    col = jnp.tile(jnp.arange(GRID_W, dtype=jnp.float32), rows)
    nf = HEAD_QK_B // 4
    inv = ROPE_BASE ** (-jnp.arange(nf, dtype=jnp.float32) / nf)
    ar = row[:, None] * inv
    ac = col[:, None] * inv
    return jnp.cos(ar), jnp.sin(ar), jnp.cos(ac), jnp.sin(ac)


def _rotate(x, cos, sin):
    h = x.shape[-1] // 2
    x1, x2 = x[..., :h], x[..., h:]
    return jnp.concatenate([x1 * cos - x2 * sin, x1 * sin + x2 * cos], axis=-1)


def apply_axial_rope(x, tables):
    cr, sr, cc, sc = [t[None, :, None, None, :] for t in tables]
    xf = x.astype(jnp.float32)
    half = x.shape[-1] // 2
    out = jnp.concatenate([_rotate(xf[..., :half], cr, sr), _rotate(xf[..., half:], cc, sc)], axis=-1)
    return out.astype(x.dtype)


def dwconv(x, w):
    k = w.shape[0]
    return lax.conv_general_dilated(x, w[:, None, :], (1,), [(k // 2, k // 2)],
                                    dimension_numbers=("NWC", "WIO", "NWC"),
                                    feature_group_count=x.shape[-1])


def chunk_gmlp(au, av, vnorm_g, w_s, b_s):
    bsz, n, _ = au.shape
    u = jax.nn.gelu(au)
    v = jax.nn.gelu(av).reshape(bsz, n // CHUNK, CHUNK, N_HEADS_A, HEAD_A)
    v = rms_norm(v, vnorm_g.reshape(N_HEADS_A, HEAD_A))
    mixed = jnp.einsum("hpq,bnqhc->bnphc", w_s, v) + b_s.T[:, :, None]
    return u * mixed.reshape(bsz, n, GROUP_W)


def qk_heads(t, g):
    bsz, n, _ = t.shape
    return rms_norm(t.reshape(bsz, n, N_HEADS_B, 2, HEAD_QK_B), g)


def v_heads(t):
    bsz, n, _ = t.shape
    return t.reshape(bsz, n, N_HEADS_B, HEAD_V_B)


def diff_attention(q, k, v, lam):
    s = jnp.einsum("bqhmd,bkhmd->bhmqk", q, k).astype(jnp.float32) * (q.shape[-1] ** -0.5)
    p = jax.nn.softmax(s, axis=-1)
    w = p[:, :, 0] - lam * p[:, :, 1]
    return jnp.einsum("bhqk,bkhe->bqhe", w.astype(v.dtype), v)


def diff_attention_blocks(q, k, v, lam):
    bsz, n = q.shape[:2]
    qb = q.reshape(bsz, n // Q_BLOCK, Q_BLOCK, *q.shape[2:]).swapaxes(0, 1)
    out = lax.map(lambda qq: diff_attention(qq, k, v, lam), qb)
    return out.swapaxes(0, 1).reshape(bsz, n, N_HEADS_B, HEAD_V_B)


def diff_post(y, g, lam_init):
    bsz, n = y.shape[:2]
    return (rms_norm(y, g) * (1.0 - lam_init)).reshape(bsz, n, GROUP_W)


def conformer_conv(ca, cg, w, b, ln_g, ln_b):
    z = ca * jax.nn.sigmoid(cg)
    z = dwconv(z, w) + b
    return jax.nn.silu(layer_norm(z, ln_g, ln_b))


def short_gated_conv(db, dc, dh, w):
    return db * dwconv(dc * dh, w)


def moe_ffn(t, w_r, b_r, w_gu, b_gu, w_dn, b_dn):
    logits = (t @ w_r + b_r).astype(jnp.float32)
    top_v, top_i = lax.top_k(logits, TOP_K)
    gates = jnp.einsum("nk,nke->ne", jax.nn.softmax(top_v, axis=-1),
                       jax.nn.one_hot(top_i, N_EXPERTS, dtype=jnp.float32))

    def expert(acc, xs):
        wgu, bgu, wdn, bdn, g = xs
        gl, up = jnp.split(t @ wgu + bgu, 2, axis=-1)
        gl = jnp.minimum(gl, SWIGLU_LIMIT)
        up = jnp.clip(up, -SWIGLU_LIMIT, SWIGLU_LIMIT)
        hid = (up + 1.0) * gl * jax.nn.sigmoid(SWIGLU_ALPHA * gl)
        return acc + g[:, None].astype(t.dtype) * (hid @ wdn + bdn), None

    out, _ = lax.scan(expert, jnp.zeros_like(t), (w_gu, b_gu, w_dn, b_dn, gates.T))
    return out


def setup_inputs(seed: int = 0) -> dict:
    key = jax.random.key(seed)
    ks = iter(jax.random.split(key, 40))

    def nrm(shape, scale):
        return jax.random.normal(next(ks), shape, jnp.float32) * scale

    L, D, G, E, F = DEPTH, D_MODEL, GROUP_W, N_EXPERTS, D_FF
    return {
        "x": nrm((BATCH, SEQ, D), 1.0),
        "c": nrm((BATCH, D), 1.0),
        "ctx": nrm((BATCH, CTX_LEN, D), 1.0),
        "c_ctx": nrm((D,), 1.0),
        "w_mod": nrm((L, D, 6 * D), 0.5 * D ** -0.5),
        "b_mod": nrm((L, 6 * D), 0.02),
        "norm1_g": 1.0 + nrm((L, D), 0.02),
        "norm2_g": 1.0 + nrm((L, D), 0.02),
        "w_in": nrm((L, D, D_IN), D ** -0.5),
        "a_vnorm_g": 1.0 + nrm((L, G), 0.02),
        "a_ws": nrm((L, N_HEADS_A, CHUNK, CHUNK), CHUNK ** -0.5),
        "a_bs": 1.0 + nrm((L, N_HEADS_A, CHUNK), 0.1),
        "b_qnorm_g": 1.0 + nrm((L, 2, HEAD_QK_B), 0.02),
        "b_knorm_g": 1.0 + nrm((L, 2, HEAD_QK_B), 0.02),
        "b_lam_q1": nrm((L, HEAD_QK_B), 0.1),
        "b_lam_k1": nrm((L, HEAD_QK_B), 0.1),
        "b_lam_q2": nrm((L, HEAD_QK_B), 0.1),
        "b_lam_k2": nrm((L, HEAD_QK_B), 0.1),
        "b_subln_g": 1.0 + nrm((L, HEAD_V_B), 0.02),
        "c_conv_w": nrm((L, CONV_C, G), CONV_C ** -0.5),
        "c_conv_b": nrm((L, G), 0.02),
        "c_ln_g": 1.0 + nrm((L, G), 0.02),
        "c_ln_b": nrm((L, G), 0.02),
        "d_conv_w": nrm((L, CONV_D, G), CONV_D ** -0.5),
        "w_out": nrm((L, D, D), D ** -0.5),
        "router_w": nrm((L, D, E), D ** -0.5),
        "router_b": nrm((L, E), 0.01),
        "exp_w_gu": nrm((L, E, D, 2 * F), D ** -0.5),
        "exp_b_gu": nrm((L, E, 2 * F), 0.01),
        "exp_w_dn": nrm((L, E, F, D), F ** -0.5),
        "exp_b_dn": nrm((L, E, D), 0.01),
    }


def reference(x, c, ctx, c_ctx, w_mod, b_mod, norm1_g, norm2_g, w_in, a_vnorm_g, a_ws, a_bs,
              b_qnorm_g, b_knorm_g, b_lam_q1, b_lam_k1, b_lam_q2, b_lam_k2, b_subln_g,
              c_conv_w, c_conv_b, c_ln_g, c_ln_b, d_conv_w, w_out, router_w, router_b,
              exp_w_gu, exp_b_gu, exp_w_dn, exp_b_dn):
    bsz, seq, dm = x.shape
    rope = axial_rope_tables(seq)
    xc = ctx
    for l in range(DEPTH):
        last = l == DEPTH - 1
        lam_init = 0.8 - 0.6 * math.exp(-0.3 * l)
        lam = (jnp.exp(jnp.sum(b_lam_q1[l] * b_lam_k1[l]).astype(jnp.float32))
               - jnp.exp(jnp.sum(b_lam_q2[l] * b_lam_k2[l]).astype(jnp.float32)) + lam_init)
        sh1, sc1, g1, sh2, sc2, g2 = jnp.split((jax.nn.silu(c) @ w_mod[l] + b_mod[l])[:, None, :], 6, axis=-1)
        sh1c, sc1c, g1c, sh2c, sc2c, g2c = jnp.split(jax.nn.silu(c_ctx) @ w_mod[l] + b_mod[l], 6)
        h = rms_norm(x, norm1_g[l]) * (1.0 + sc1) + sh1
        hc = rms_norm(xc, norm1_g[l]) * (1.0 + sc1c) + sh1c
        au, av, bq, bk, bv, ca, cg, db, dc, dh = jnp.split(h @ w_in[l], N_SPLITS, axis=-1)
        if last:
            bkc = hc @ w_in[l][:, 3 * GROUP_W:4 * GROUP_W]
            bvc = hc @ w_in[l][:, 4 * GROUP_W:5 * GROUP_W]
        else:
            auc, avc, bqc, bkc, bvc, cac, cgc, dbc, dcc, dhc = jnp.split(hc @ w_in[l], N_SPLITS, axis=-1)
        kc = qk_heads(bkc, b_knorm_g[l])
        vc = v_heads(bvc)
        q = apply_axial_rope(qk_heads(bq, b_qnorm_g[l]), rope)
        k = apply_axial_rope(qk_heads(bk, b_knorm_g[l]), rope)
        yb = diff_attention_blocks(q, jnp.concatenate([k, kc], axis=1),
                                   jnp.concatenate([v_heads(bv), vc], axis=1), lam)
        y = jnp.concatenate([
            chunk_gmlp(au, av, a_vnorm_g[l], a_ws[l], a_bs[l]),
            diff_post(yb, b_subln_g[l], lam_init),
            conformer_conv(ca, cg, c_conv_w[l], c_conv_b[l], c_ln_g[l], c_ln_b[l]),
            short_gated_conv(db, dc, dh, d_conv_w[l]),
        ], axis=-1)
        x = x + g1 * (y @ w_out[l])
        h2 = rms_norm(x, norm2_g[l]) * (1.0 + sc2) + sh2
        moe_args = (router_w[l], router_b[l], exp_w_gu[l], exp_b_gu[l], exp_w_dn[l], exp_b_dn[l])
        if last:
            x = x + g2 * moe_ffn(h2.reshape(-1, dm), *moe_args).reshape(x.shape)
        else:
            ycb = diff_attention(qk_heads(bqc, b_qnorm_g[l]), kc, vc, lam)
            yc = jnp.concatenate([
                chunk_gmlp(auc, avc, a_vnorm_g[l], a_ws[l], a_bs[l]),
                diff_post(ycb, b_subln_g[l], lam_init),
                conformer_conv(cac, cgc, c_conv_w[l], c_conv_b[l], c_ln_g[l], c_ln_b[l]),
                short_gated_conv(dbc, dcc, dhc, d_conv_w[l]),
            ], axis=-1)
            xc = xc + g1c * (yc @ w_out[l])
            h2c = rms_norm(xc, norm2_g[l]) * (1.0 + sc2c) + sh2c
            ff = moe_ffn(jnp.concatenate([h2.reshape(-1, dm), h2c.reshape(-1, dm)], axis=0), *moe_args)
            x = x + g2 * ff[:bsz * seq].reshape(x.shape)
            xc = xc + g2c * ff[bsz * seq:].reshape(xc.shape)
    return x
```

```python
import functools
import math

import jax
import jax.numpy as jnp
from jax import lax
from jax.experimental import pallas as pl
from jax.experimental.pallas import tpu as pltpu

F32 = jnp.float32
BF16 = jnp.bfloat16

GROUP_W = 256
N_SPLITS = 10
N_HEADS_A = 4
HEAD_A = 64
CHUNK = 128
N_HEADS_B = 4
HEAD_V_B = 64
HEAD_QK_B = 32
GRID_W = 64
ROPE_BASE = 10000.0
CONV_C = 31
CONV_D = 3
N_EXPERTS = 32
TOP_K = 4
SWIGLU_LIMIT = 7.0
SWIGLU_ALPHA = 1.702
EPS = 1e-6

GATE_LANES = 128
CONV_HALO = 16
VMEM_LIMIT = 56 * 1024 * 1024


def _cparams(*sem):
    return pltpu.CompilerParams(dimension_semantics=sem, vmem_limit_bytes=VMEM_LIMIT)


def _group_ones(width, group):
    i = jnp.arange(width) // group
    return (i[:, None] == i[None, :]).astype(BF16)


def _mod_kernel(c_ref, w_ref, b_ref, o_ref):
    c = c_ref[...]
    a = (c * jax.nn.sigmoid(c)).astype(BF16)
    o_ref[0] = jnp.dot(a, w_ref[0].astype(BF16), preferred_element_type=F32) + b_ref[0]


def _modulation(cc, w_mod, b_mod):
    nl, d, d6 = w_mod.shape
    r = cc.shape[0]
    return pl.pallas_call(
        _mod_kernel,
        grid=(nl, d6 // d),
        in_specs=[pl.BlockSpec((r, d), lambda l, j: (0, 0)),
                  pl.BlockSpec((1, d, d), lambda l, j: (l, 0, j)),
                  pl.BlockSpec((1, 1, d), lambda l, j: (l, 0, j))],
        out_specs=pl.BlockSpec((1, r, d), lambda l, j: (l, 0, j)),
        out_shape=jax.ShapeDtypeStruct((nl, r, d6), F32),
        compiler_params=_cparams("arbitrary", "arbitrary"),
        name="modulation",
    )(cc, w_mod, b_mod.reshape(nl, 1, d6))


def _rope_norm(t, gsum, gain, cos, sin, lo_lane):
    ss = jnp.dot((t * t).astype(BF16), gsum, preferred_element_type=F32)
    tn = t * lax.rsqrt(ss * (1.0 / HEAD_QK_B) + EPS) * gain
    rot = jnp.where(lo_lane, pltpu.roll(tn, GROUP_W - 8, 1), pltpu.roll(tn, 8, 1))
    return tn * cos + rot * sin


def _in_kernel(x_ref, sc_ref, sh_ref, g_ref, w_ref, gsum_ref, qg_ref, kg_ref,
               cq_ref, sq_ref, ck_ref, sk_ref, o_ref):
    x = x_ref[0]
    ms = jnp.mean(x * x, axis=-1, keepdims=True)
    h = (x * lax.rsqrt(ms + EPS) * g_ref[...]) * (1.0 + sc_ref[0]) + sh_ref[0]
    p = jnp.dot(h.astype(BF16), w_ref[...], preferred_element_type=F32)
    g = GROUP_W
    lane = lax.broadcasted_iota(jnp.int32, (1, g), 1)
    lo_lane = (lane & 8) == 0
    gsum = gsum_ref[...]
    q = _rope_norm(p[:, 2 * g:3 * g], gsum, qg_ref[...], cq_ref[...], sq_ref[...], lo_lane)
    k = _rope_norm(p[:, 3 * g:4 * g], gsum, kg_ref[...], ck_ref[...], sk_ref[...], lo_lane)
    o_ref[0, :, 0:2 * g] = p[:, 0:2 * g].astype(BF16)
    o_ref[0, :, 2 * g:3 * g] = q.astype(BF16)
    o_ref[0, :, 3 * g:4 * g] = k.astype(BF16)
    o_ref[0, :, 4 * g:] = p[:, 4 * g:].astype(BF16)


def _in_proj(x, sc, sh, g, w_bf, gsum32, qg, kg, tabs, tm):
    bn, sn, d = x.shape
    d_in = w_bf.shape[1]
    tab_spec = pl.BlockSpec((tm, GROUP_W), lambda b, s: (s, 0))
    vec_spec = pl.BlockSpec((1, 1, d), lambda b, s: (b, 0, 0))
    row256 = pl.BlockSpec((1, GROUP_W), lambda b, s: (0, 0))
    return pl.pallas_call(
        _in_kernel,
        grid=(bn, sn // tm),
        in_specs=[pl.BlockSpec((1, tm, d), lambda b, s: (b, s, 0)),
                  vec_spec, vec_spec,
                  pl.BlockSpec((1, d), lambda b, s: (0, 0)),
                  pl.BlockSpec((d, d_in), lambda b, s: (0, 0)),
                  pl.BlockSpec((GROUP_W, GROUP_W), lambda b, s: (0, 0)),
                  row256, row256, tab_spec, tab_spec, tab_spec, tab_spec],
        out_specs=pl.BlockSpec((1, tm, d_in), lambda b, s: (b, s, 0)),
        out_shape=jax.ShapeDtypeStruct((bn, sn, d_in), BF16),
        compiler_params=_cparams("parallel", "arbitrary"),
        name="in_proj",
    )(x, sc, sh, g, w_bf, gsum32, qg, kg, *tabs)


def _mix_kernel(au_ref, av_ref, ca_ref, cg_ref, db_ref, dc_ref, dh_ref,
                vg_ref, ws_ref, bs_ref, gsum_ref, cw_ref, cb_ref, lng_ref, lnb_ref, dw_ref,
                ya_ref, yc_ref, yd_ref, zc_ref, zd_ref, *, sn):
    g = GROUP_W
    lane = lax.broadcasted_iota(jnp.int32, (1, g), 1)
    n_chunks = sn // CHUNK

    def gmlp_chunk(ci, carry):
        r0 = pl.multiple_of(ci * CHUNK, CHUNK)
        u = jax.nn.gelu(au_ref[0, pl.ds(r0, CHUNK), :].astype(F32))
        v = jax.nn.gelu(av_ref[0, pl.ds(r0, CHUNK), :].astype(F32))
        ss = jnp.dot((v * v).astype(BF16), gsum_ref[...], preferred_element_type=F32)
        vn = (v * lax.rsqrt(ss * (1.0 / HEAD_A) + EPS) * vg_ref[...]).astype(BF16)
        mixed = bs_ref[...]
        for h in range(N_HEADS_A):
            mh = jnp.dot(ws_ref[h], vn, preferred_element_type=F32)
            hm = (lane >= h * HEAD_A) & (lane < (h + 1) * HEAD_A)
            mixed = mixed + jnp.where(hm, mh, 0.0)
        ya_ref[0, pl.ds(r0, CHUNK), :] = (u * mixed).astype(BF16)
        return carry

    lax.fori_loop(0, n_chunks, gmlp_chunk, 0)

    zeros_halo = jnp.zeros((CONV_HALO, g), F32)
    zc_ref[0:CONV_HALO, :] = zeros_halo
    zc_ref[CONV_HALO + sn:CONV_HALO + sn + CONV_HALO, :] = zeros_halo
    zd_ref[0:CONV_HALO, :] = zeros_halo
    zd_ref[CONV_HALO + sn:CONV_HALO + sn + CONV_HALO, :] = zeros_halo

    def stage_chunk(ci, carry):
        r0 = pl.multiple_of(ci * CHUNK, CHUNK)
        ca = ca_ref[0, pl.ds(r0, CHUNK), :].astype(F32)
        cg = cg_ref[0, pl.ds(r0, CHUNK), :].astype(F32)
        zc_ref[pl.ds(r0 + CONV_HALO, CHUNK), :] = ca * jax.nn.sigmoid(cg)
        dc = dc_ref[0, pl.ds(r0, CHUNK), :].astype(F32)
        dh = dh_ref[0, pl.ds(r0, CHUNK), :].astype(F32)
        zd_ref[pl.ds(r0 + CONV_HALO, CHUNK), :] = dc * dh
        return carry

    lax.fori_loop(0, n_chunks, stage_chunk, 0)

    def conv_chunk(ci, carry):
        r0 = pl.multiple_of(ci * CHUNK, CHUNK)
        win = zc_ref[pl.ds(r0, CHUNK + 2 * CONV_HALO), :]
        acc = jnp.zeros((CHUNK, g), F32) + cb_ref[...]
        base = CONV_HALO - CONV_C // 2
        for k in range(CONV_C):
            acc = acc + cw_ref[k:k + 1, :] * win[base + k:base + k + CHUNK, :]
        mu = jnp.mean(acc, axis=-1, keepdims=True)
        dev = acc - mu
        var = jnp.mean(dev * dev, axis=-1, keepdims=True)
        yln = dev * lax.rsqrt(var + EPS) * lng_ref[...] + lnb_ref[...]
        yc_ref[0, pl.ds(r0, CHUNK), :] = (yln * jax.nn.sigmoid(yln)).astype(BF16)

        wind = zd_ref[pl.ds(r0, CHUNK + 2 * CONV_HALO), :]
        accd = jnp.zeros((CHUNK, g), F32)
        based = CONV_HALO - CONV_D // 2
        for k in range(CONV_D):
            accd = accd + dw_ref[k:k + 1, :] * wind[based + k:based + k + CHUNK, :]
        db = db_ref[0, pl.ds(r0, CHUNK), :].astype(F32)
        yd_ref[0, pl.ds(r0, CHUNK), :] = (db * accd).astype(BF16)
        return carry

    lax.fori_loop(0, n_chunks, conv_chunk, 0)


def _mixers(p, vg, ws_bf, bs_exp, gsum64, cw, cb, lng, lnb, dw):
    bn, sn, _ = p.shape
    g = GROUP_W

    def col(j):
        return pl.BlockSpec((1, sn, g), lambda b, j=j: (b, 0, j))

    def full(a):
        nd = a.ndim
        return pl.BlockSpec(a.shape, lambda b, nd=nd: (0,) * nd)

    out_spec = pl.BlockSpec((1, sn, g), lambda b: (b, 0, 0))
    out_sds = jax.ShapeDtypeStruct((bn, sn, g), BF16)
    params = (vg, ws_bf, bs_exp, gsum64, cw, cb, lng, lnb, dw)
    return pl.pallas_call(
        functools.partial(_mix_kernel, sn=sn),
        grid=(bn,),
        in_specs=[col(0), col(1), col(5), col(6), col(7), col(8), col(9)] + [full(a) for a in params],
        out_specs=[out_spec, out_spec, out_spec],
        out_shape=[out_sds, out_sds, out_sds],
        scratch_shapes=[pltpu.VMEM((sn + 2 * CONV_HALO, g), F32),
                        pltpu.VMEM((sn + 2 * CONV_HALO, g), F32)],
        compiler_params=_cparams("parallel"),
        name="mixers",
    )(p, p, p, p, p, p, p, *params)


def _attn_kernel(lam_ref, q_ref, k_ref, v_ref, gpost_ref, gsum_ref, o_ref):
    lam = lam_ref[0]
    q = q_ref[0]
    k = k_ref[0]
    v = v_ref[0]
    tq = q.shape[0]
    lane = lax.broadcasted_iota(jnp.int32, (1, GROUP_W), 1)
    y = jnp.zeros((tq, GROUP_W), F32)
    for h in range(N_HEADS_B):
        probs = []
        for m in range(2):
            lo = h * HEAD_V_B + m * HEAD_QK_B
            qm = jnp.where((lane >= lo) & (lane < lo + HEAD_QK_B), q, jnp.zeros_like(q))
            s = lax.dot_general(qm, k, (((1,), (1,)), ((), ())), preferred_element_type=F32)
            e = jnp.exp(s - jnp.max(s, axis=-1, keepdims=True))
            probs.append(e * (1.0 / jnp.sum(e, axis=-1, keepdims=True)))
        w = probs[0] - lam * probs[1]
        o = jnp.dot(w.astype(BF16), v, preferred_element_type=F32)
        y = jnp.where((lane >= h * HEAD_V_B) & (lane < (h + 1) * HEAD_V_B), o, y)
    ss = jnp.dot((y * y).astype(BF16), gsum_ref[...], preferred_element_type=F32)
    o_ref[0] = (y * lax.rsqrt(ss * (1.0 / HEAD_V_B) + EPS) * gpost_ref[...]).astype(BF16)


def _attention(lam, p_q, k_all, v_all, gpost, gsum64, tq):
    bn, sq, _ = p_q.shape
    sk = k_all.shape[1]
    g = GROUP_W
    return pl.pallas_call(
        _attn_kernel,
        grid=(bn, sq // tq),
        in_specs=[pl.BlockSpec(memory_space=pltpu.SMEM),
                  pl.BlockSpec((1, tq, g), lambda b, i: (b, i, 2)),
                  pl.BlockSpec((1, sk, g), lambda b, i: (b, 0, 0)),
                  pl.BlockSpec((1, sk, g), lambda b, i: (b, 0, 0)),
                  pl.BlockSpec((1, g), lambda b, i: (0, 0)),
                  pl.BlockSpec((g, g), lambda b, i: (0, 0))],
        out_specs=pl.BlockSpec((1, tq, g), lambda b, i: (b, i, 0)),
        out_shape=jax.ShapeDtypeStruct((bn, sq, g), BF16),
        compiler_params=_cparams("parallel", "arbitrary"),
        name="diff_attention",
    )(lam, p_q, k_all, v_all, gpost, gsum64)


def _out_kernel(ya_ref, yb_ref, yc_ref, yd_ref, x_ref, g1_ref, sc2_ref, sh2_ref, n2g_ref,
                wo_ref, wrh_ref, wrl_ref, br_ref, xo_ref, h2_ref, gate_ref):
    g = GROUP_W
    acc = jnp.dot(ya_ref[0], wo_ref[0:g, :], preferred_element_type=F32)
    acc = acc + jnp.dot(yb_ref[0], wo_ref[g:2 * g, :], preferred_element_type=F32)
    acc = acc + jnp.dot(yc_ref[0], wo_ref[2 * g:3 * g, :], preferred_element_type=F32)
    acc = acc + jnp.dot(yd_ref[0], wo_ref[3 * g:4 * g, :], preferred_element_type=F32)
    xn = x_ref[0] + g1_ref[0] * acc
    xo_ref[0] = xn
    ms = jnp.mean(xn * xn, axis=-1, keepdims=True)
    h2 = (xn * lax.rsqrt(ms + EPS) * n2g_ref[...]) * (1.0 + sc2_ref[0]) + sh2_ref[0]
    hh = h2.astype(BF16)
    h2_ref[0] = hh
    hl = (h2 - hh.astype(F32)).astype(BF16)
    logits = (jnp.dot(hh, wrh_ref[...], preferred_element_type=F32)
              + jnp.dot(hl, wrh_ref[...], preferred_element_type=F32)
              + jnp.dot(hh, wrl_ref[...], preferred_element_type=F32)) + br_ref[...]
    tm = logits.shape[0]
    lane = lax.broadcasted_iota(jnp.int32, (tm, GATE_LANES), 1)
    neg_inf = jnp.float32(-jnp.inf)
    l = jnp.where(lane < N_EXPERTS, logits, neg_inf)
    vals, sels = [], []
    for _ in range(TOP_K):
        m = jnp.max(l, axis=-1, keepdims=True)
        idx = jnp.min(jnp.where(l == m, lane, GATE_LANES), axis=-1, keepdims=True)
        sel = lane == idx
        vals.append(m)
        sels.append(sel)
        l = jnp.where(sel, neg_inf, l)
    es = [jnp.exp(vk - vals[0]) for vk in vals]
    inv = 1.0 / (es[0] + es[1] + es[2] + es[3])
    gates = jnp.zeros((tm, GATE_LANES), F32)
    for sel, ek in zip(sels, es):
        gates = jnp.where(sel, ek * inv, gates)
    gate_ref[0] = gates


def _out_proj(ya, yb, yc, yd, x, g1, sc2, sh2, n2g, wo_bf, wr_hi, wr_lo, br, tm):
    bn, sn, d = x.shape
    g = GROUP_W
    yspec = pl.BlockSpec((1, tm, g), lambda b, s: (b, s, 0))
    vec_spec = pl.BlockSpec((1, 1, d), lambda b, s: (b, 0, 0))
    xspec = pl.BlockSpec((1, tm, d), lambda b, s: (b, s, 0))
    return pl.pallas_call(
        _out_kernel,
        grid=(bn, sn // tm),
        in_specs=[yspec, yspec, yspec, yspec, xspec, vec_spec, vec_spec, vec_spec,
                  pl.BlockSpec((1, d), lambda b, s: (0, 0)),
                  pl.BlockSpec((d, d), lambda b, s: (0, 0)),
                  pl.BlockSpec((d, GATE_LANES), lambda b, s: (0, 0)),
                  pl.BlockSpec((d, GATE_LANES), lambda b, s: (0, 0)),
                  pl.BlockSpec((1, GATE_LANES), lambda b, s: (0, 0))],
        out_specs=[xspec, xspec, pl.BlockSpec((1, tm, GATE_LANES), lambda b, s: (b, s, 0))],
        out_shape=[jax.ShapeDtypeStruct((bn, sn, d), F32),
                   jax.ShapeDtypeStruct((bn, sn, d), BF16),
                   jax.ShapeDtypeStruct((bn, sn, GATE_LANES), F32)],
        compiler_params=_cparams("parallel", "arbitrary"),
        name="out_proj",
    )(ya, yb, yc, yd, x, g1, sc2, sh2, n2g, wo_bf, wr_hi, wr_lo, br)


def _expert_kernel(te_ref, tf_ref, x_ref, g_ref, wgu_ref, bgu_ref, wdn_ref, bdn_ref, y_ref,
                   wgu_s, wdn_s):
    i = pl.program_id(0)
    flag = tf_ref[i]
    d, f2 = wgu_s.shape
    f = f2 // 2
    rows = 128

    @pl.when((flag & 2) != 0)
    def _():
        def cast_gu(r, c):
            r0 = pl.multiple_of(r * rows, rows)
            wgu_s[pl.ds(r0, rows), :] = wgu_ref[0, pl.ds(r0, rows), :].astype(BF16)
            return c
        lax.fori_loop(0, d // rows, cast_gu, 0)

        def cast_dn(r, c):
            r0 = pl.multiple_of(r * rows, rows)
            wdn_s[pl.ds(r0, rows), :] = wdn_ref[0, pl.ds(r0, rows), :].astype(BF16)
            return c
        lax.fori_loop(0, f // rows, cast_dn, 0)

    @pl.when((flag & 1) != 0)
    def _():
        gu = jnp.dot(x_ref[...], wgu_s[...], preferred_element_type=F32) + bgu_ref[0]
        gl = jnp.minimum(gu[:, :f], SWIGLU_LIMIT)
        up = jnp.clip(gu[:, f:], -SWIGLU_LIMIT, SWIGLU_LIMIT)
        hid = (up + 1.0) * gl * jax.nn.sigmoid(SWIGLU_ALPHA * gl)
        y = jnp.dot(hid.astype(BF16), wdn_s[...], preferred_element_type=F32) + bdn_ref[0]
        y_ref[...] = (g_ref[...] * y).astype(y_ref.dtype)

    @pl.when((flag & 1) == 0)
    def _():
        y_ref[...] = jnp.zeros_like(y_ref)


def _experts(tile_expert, tile_flag, xs, row_gate, w_gu, b_gu, w_dn, b_dn, tm):
    rtot, d = xs.shape
    ne, _, f2 = w_gu.shape
    f = f2 // 2
    grid_spec = pltpu.PrefetchScalarGridSpec(
        num_scalar_prefetch=2,
        grid=(rtot // tm,),
        in_specs=[pl.BlockSpec((tm, d), lambda i, te, tf: (i, 0)),
                  pl.BlockSpec((tm, 1), lambda i, te, tf: (i, 0)),
                  pl.BlockSpec((1, d, f2), lambda i, te, tf: (te[i], 0, 0)),
                  pl.BlockSpec((1, 1, f2), lambda i, te, tf: (te[i], 0, 0)),
                  pl.BlockSpec((1, f, d), lambda i, te, tf: (te[i], 0, 0)),
                  pl.BlockSpec((1, 1, d), lambda i, te, tf: (te[i], 0, 0))],
        out_specs=pl.BlockSpec((tm, d), lambda i, te, tf: (i, 0)),
        scratch_shapes=[pltpu.VMEM((d, f2), BF16), pltpu.VMEM((f, d), BF16)],
    )
    return pl.pallas_call(
        _expert_kernel,
        grid_spec=grid_spec,
        out_shape=jax.ShapeDtypeStruct((rtot, d), BF16),
        compiler_params=_cparams("arbitrary"),
        name="experts",
    )(tile_expert, tile_flag, xs, row_gate, w_gu, b_gu.reshape(ne, 1, f2), w_dn, b_dn.reshape(ne, 1, d))


def _moe(h2, gates, w_gu, b_gu, w_dn, b_dn, tm):
    n, d = h2.shape
    ne = N_EXPERTS
    rtot = ((n * TOP_K + ne * (tm - 1)) // tm + 1) * tm
    n_tiles = rtot // tm
    g = gates[:, :ne]
    mask = g > 0.0
    counts = jnp.sum(mask.astype(jnp.int32), axis=0)
    padded = ((counts + tm - 1) // tm) * tm
    ends = jnp.cumsum(padded)
    offs = ends - padded
    rank = jnp.cumsum(mask.astype(jnp.int32), axis=0) - 1
    pos = jnp.where(mask, offs[None, :] + rank, rtot)
    tok = jnp.broadcast_to(jnp.arange(n, dtype=jnp.int32)[:, None], (n, ne))
    row_token = jnp.zeros((rtot,), jnp.int32).at[pos.reshape(-1)].set(tok.reshape(-1), mode="drop")
    row_gate = jnp.zeros((rtot,), F32).at[pos.reshape(-1)].set(g.reshape(-1), mode="drop")
    tile_start = jnp.arange(n_tiles, dtype=jnp.int32) * tm
    te = jnp.minimum(jnp.searchsorted(ends, tile_start, side="right").astype(jnp.int32), ne - 1)
    active = tile_start < ends[-1]
    first = active & (tile_start == offs[te])
    last_te = te[jnp.maximum(ends[-1] // tm - 1, 0)]
    te = jnp.where(active, te, last_te)
    tf = active.astype(jnp.int32) + 2 * first.astype(jnp.int32)
    xs = jnp.take(h2, row_token, axis=0)
    ys = _experts(te, tf, xs, row_gate[:, None], w_gu, b_gu, w_dn, b_dn, tm)
    pos4 = jnp.sort(pos, axis=1)[:, :TOP_K]
    ys_pad = jnp.concatenate([ys, jnp.zeros((1, d), ys.dtype)], axis=0)
    return jnp.sum(jnp.take(ys_pad, pos4, axis=0).astype(F32), axis=1)


def _rope_tables(sn, scale_q, roped):
    if not roped:
        ones = jnp.ones((sn, GROUP_W), F32)
        zeros = jnp.zeros((sn, GROUP_W), F32)
        return ones * scale_q, zeros, ones, zeros
    pos = jnp.arange(sn)
    row = (pos // GRID_W).astype(F32)
    col = (pos % GRID_W).astype(F32)
    nf = HEAD_QK_B // 4
    inv = ROPE_BASE ** (-jnp.arange(nf, dtype=F32) / nf)
    ar = row[:, None] * inv
    ac = col[:, None] * inv
    cos32 = jnp.concatenate([jnp.cos(ar), jnp.cos(ar), jnp.cos(ac), jnp.cos(ac)], axis=1)
    sin32 = jnp.concatenate([-jnp.sin(ar), jnp.sin(ar), -jnp.sin(ac), jnp.sin(ac)], axis=1)
    reps = GROUP_W // HEAD_QK_B
    cos = jnp.tile(cos32, (1, reps))
    sin = jnp.tile(sin32, (1, reps))
    return cos * scale_q, sin * scale_q, cos, sin


def kernel(x, c, ctx, c_ctx, w_mod, b_mod, norm1_g, norm2_g, w_in, a_vnorm_g, a_ws, a_bs, b_qnorm_g, b_knorm_g, b_lam_q1, b_lam_k1, b_lam_q2, b_lam_k2, b_subln_g, c_conv_w, c_conv_b, c_ln_g, c_ln_b, d_conv_w, w_out, router_w, router_b, exp_w_gu, exp_b_gu, exp_w_dn, exp_b_dn):
    bsz, seq, dm = x.shape
    ctx_len = ctx.shape[1]
    depth = w_in.shape[0]
    g = GROUP_W
    tm_lat = min(512, seq)
    tm_ctx = min(512, ctx_len)
    tq_lat = min(256, seq)
    tq_ctx = min(256, ctx_len)
    tm_moe = 256

    n_mod_rows = -(-(bsz + 1) // 8) * 8
    cc = jnp.concatenate([c, c_ctx[None, :], jnp.zeros((n_mod_rows - bsz - 1, dm), F32)], axis=0)
    mods = _modulation(cc, w_mod, b_mod)

    gsum32 = _group_ones(g, HEAD_QK_B)
    gsum64 = _group_ones(g, HEAD_V_B)
    scale_q = HEAD_QK_B ** -0.5
    tabs_lat = _rope_tables(seq, scale_q, True)
    tabs_ctx = _rope_tables(ctx_len, scale_q, False)

    xc = ctx
    for l in range(depth):
        last = l == depth - 1
        lam_init = 0.8 - 0.6 * math.exp(-0.3 * l)
        lam = (jnp.exp(jnp.sum(b_lam_q1[l] * b_lam_k1[l])) - jnp.exp(jnp.sum(b_lam_q2[l] * b_lam_k2[l]))
               + lam_init).astype(F32).reshape(1)
        m_lat = mods[l, :bsz].reshape(bsz, 6, 1, dm)
        m_ctx = jnp.broadcast_to(mods[l, bsz].reshape(1, 6, 1, dm), (bsz, 6, 1, dm))
        sh1, sc1, g1, sh2, sc2, g2 = [m_lat[:, i] for i in range(6)]
        sh1c, sc1c, g1c, sh2c, sc2c, g2c = [m_ctx[:, i] for i in range(6)]

        w_in_bf = w_in[l].astype(BF16)
        n1g = norm1_g[l][None, :]
        n2g = norm2_g[l][None, :]
        reps = g // HEAD_QK_B // 2
        qg = jnp.tile(b_qnorm_g[l].reshape(1, 2 * HEAD_QK_B), (1, reps))
        kg = jnp.tile(b_knorm_g[l].reshape(1, 2 * HEAD_QK_B), (1, reps))
        gpost = jnp.tile(b_subln_g[l][None, :], (1, N_HEADS_B)) * (1.0 - lam_init)
        vg = a_vnorm_g[l][None, :]
        ws_bf = a_ws[l].astype(BF16)
        bs_exp = jnp.repeat(a_bs[l].T, HEAD_A, axis=1)
        mix_params = (vg, ws_bf, bs_exp, gsum64, c_conv_w[l], c_conv_b[l][None, :], c_ln_g[l][None, :],
                      c_ln_b[l][None, :], d_conv_w[l])
        wo_bf = w_out[l].astype(BF16)
        wr = jnp.pad(router_w[l], ((0, 0), (0, GATE_LANES - N_EXPERTS)))
        wr_hi = wr.astype(BF16)
        wr_lo = (wr - wr_hi.astype(F32)).astype(BF16)
        br = jnp.pad(router_b[l], (0, GATE_LANES - N_EXPERTS))[None, :]

        p_lat = _in_proj(x, sc1, sh1, n1g, w_in_bf, gsum32, qg, kg, tabs_lat, tm_lat)
        p_ctx = _in_proj(xc, sc1c, sh1c, n1g, w_in_bf, gsum32, qg, kg, tabs_ctx, tm_ctx)

        k_all = jnp.concatenate([p_lat[:, :, 3 * g:4 * g], p_ctx[:, :, 3 * g:4 * g]], axis=1)
        v_all = jnp.concatenate([p_lat[:, :, 4 * g:5 * g], p_ctx[:, :, 4 * g:5 * g]], axis=1)
        yb = _attention(lam, p_lat, k_all, v_all, gpost, gsum64, tq_lat)
        ya, yc, yd = _mixers(p_lat, *mix_params)
        x, h2, gates = _out_proj(ya, yb, yc, yd, x, g1, sc2, sh2, n2g, wo_bf, wr_hi, wr_lo, br, tm_lat)
        moe_w = (exp_w_gu[l], exp_b_gu[l], exp_w_dn[l], exp_b_dn[l])
        if last:
            ff = _moe(h2.reshape(-1, dm), gates.reshape(-1, GATE_LANES), *moe_w, tm_moe)
            x = x + g2 * ff.reshape(x.shape)
        else:
            ybc = _attention(lam, p_ctx, p_ctx[:, :, 3 * g:4 * g], p_ctx[:, :, 4 * g:5 * g], gpost, gsum64, tq_ctx)
            yac, ycc, ydc = _mixers(p_ctx, *mix_params)
            xc, h2c, gatesc = _out_proj(yac, ybc, ycc, ydc, xc, g1c, sc2c, sh2c, n2g, wo_bf, wr_hi, wr_lo, br,
                                        tm_ctx)
            h2_all = jnp.concatenate([h2.reshape(-1, dm), h2c.reshape(-1, dm)], axis=0)
            gates_all = jnp.concatenate([gates.reshape(-1, GATE_LANES), gatesc.reshape(-1, GATE_LANES)], axis=0)
            ff = _moe(h2_all, gates_all, *moe_w, tm_moe)
            x = x + g2 * ff[:bsz * seq].reshape(x.shape)
            xc = xc + g2c * ff[bsz * seq:].reshape(xc.shape)
    return x
```

```python
import functools
import math

import jax
import jax.numpy as jnp
from jax import lax
from jax.experimental import pallas as pl
from jax.experimental.pallas import tpu as pltpu

F32 = jnp.float32
BF16 = jnp.bfloat16

GROUP_W = 256
N_SPLITS = 10
N_HEADS_A = 4
HEAD_A = 64
CHUNK = 128
N_HEADS_B = 4
HEAD_V_B = 64
HEAD_QK_B = 32
GRID_W = 64
ROPE_BASE = 10000.0
CONV_C = 31
CONV_D = 3
N_EXPERTS = 32
TOP_K = 4
SWIGLU_LIMIT = 7.0
SWIGLU_ALPHA = 1.702
EPS = 1e-6

GATE_LANES = 128
CONV_HALO = 16
VMEM_LIMIT = 56 * 1024 * 1024
ATTN_LOOKAHEAD = 2

TOKEN_BLOCK = 256
CHUNK_ROWS = 16
ROW_GROUP = 256
LOCAL_ROWS = -(-(TOKEN_BLOCK * TOP_K + N_EXPERTS * (CHUNK_ROWS - 1)) // ROW_GROUP) * ROW_GROUP
MAX_CHUNKS = LOCAL_ROWS // CHUNK_ROWS
TABLE_W = 128


def _cparams(*sem):
    return pltpu.CompilerParams(dimension_semantics=sem, vmem_limit_bytes=VMEM_LIMIT)


def _group_ones(width, group):
    i = jnp.arange(width) // group
    return (i[:, None] == i[None, :]).astype(BF16)


def _mod_kernel(c_ref, w_ref, b_ref, o_ref):
    c = c_ref[...]
    a = (c * jax.nn.sigmoid(c)).astype(BF16)
    o_ref[0] = jnp.dot(a, w_ref[0].astype(BF16), preferred_element_type=F32) + b_ref[0]


def _modulation(cc, w_mod, b_mod):
    nl, d, d6 = w_mod.shape
    r = cc.shape[0]
    return pl.pallas_call(
        _mod_kernel,
        grid=(nl, d6 // d),
        in_specs=[pl.BlockSpec((r, d), lambda l, j: (0, 0)),
                  pl.BlockSpec((1, d, d), lambda l, j: (l, 0, j)),
                  pl.BlockSpec((1, 1, d), lambda l, j: (l, 0, j))],
        out_specs=pl.BlockSpec((1, r, d), lambda l, j: (l, 0, j)),
        out_shape=jax.ShapeDtypeStruct((nl, r, d6), F32),
        compiler_params=_cparams("arbitrary", "arbitrary"),
        name="modulation",
    )(cc, w_mod, b_mod.reshape(nl, 1, d6))


def _rope_norm(t, gsum, gain, cos, sin, lo_lane):
    ss = jnp.dot((t * t).astype(BF16), gsum, preferred_element_type=F32)
    tn = t * lax.rsqrt(ss * (1.0 / HEAD_QK_B) + EPS) * gain
    rot = jnp.where(lo_lane, pltpu.roll(tn, GROUP_W - 8, 1), pltpu.roll(tn, 8, 1))
    return tn * cos + rot * sin


def _in_kernel(x_ref, sc_ref, sh_ref, g_ref, w_ref, gsum_ref, qg_ref, kg_ref,
               cq_ref, sq_ref, ck_ref, sk_ref, o_ref):
    x = x_ref[0]
    ms = jnp.mean(x * x, axis=-1, keepdims=True)
    h = (x * lax.rsqrt(ms + EPS) * g_ref[...]) * (1.0 + sc_ref[0]) + sh_ref[0]
    p = jnp.dot(h.astype(BF16), w_ref[...], preferred_element_type=F32)
    g = GROUP_W
    lane = lax.broadcasted_iota(jnp.int32, (1, g), 1)
    lo_lane = (lane & 8) == 0
    gsum = gsum_ref[...]
    q = _rope_norm(p[:, 2 * g:3 * g], gsum, qg_ref[...], cq_ref[...], sq_ref[...], lo_lane)
    k = _rope_norm(p[:, 3 * g:4 * g], gsum, kg_ref[...], ck_ref[...], sk_ref[...], lo_lane)
    o_ref[0, :, 0:2 * g] = p[:, 0:2 * g].astype(BF16)
    o_ref[0, :, 2 * g:3 * g] = q.astype(BF16)
    o_ref[0, :, 3 * g:4 * g] = k.astype(BF16)
    o_ref[0, :, 4 * g:] = p[:, 4 * g:].astype(BF16)


def _in_proj(x, sc, sh, g, w_bf, gsum32, qg, kg, tabs, tm):
    bn, sn, d = x.shape
    d_in = w_bf.shape[1]
    tab_spec = pl.BlockSpec((tm, GROUP_W), lambda b, s: (s, 0))
    vec_spec = pl.BlockSpec((1, 1, d), lambda b, s: (b, 0, 0))
    row256 = pl.BlockSpec((1, GROUP_W), lambda b, s: (0, 0))
    return pl.pallas_call(
        _in_kernel,
        grid=(bn, sn // tm),
        in_specs=[pl.BlockSpec((1, tm, d), lambda b, s: (b, s, 0)),
                  vec_spec, vec_spec,
                  pl.BlockSpec((1, d), lambda b, s: (0, 0)),
                  pl.BlockSpec((d, d_in), lambda b, s: (0, 0)),
                  pl.BlockSpec((GROUP_W, GROUP_W), lambda b, s: (0, 0)),
                  row256, row256, tab_spec, tab_spec, tab_spec, tab_spec],
        out_specs=pl.BlockSpec((1, tm, d_in), lambda b, s: (b, s, 0)),
        out_shape=jax.ShapeDtypeStruct((bn, sn, d_in), BF16),
        compiler_params=_cparams("parallel", "arbitrary"),
        name="in_proj",
    )(x, sc, sh, g, w_bf, gsum32, qg, kg, *tabs)


def _mix_kernel(au_ref, av_ref, ca_ref, cg_ref, db_ref, dc_ref, dh_ref,
                vg_ref, ws_ref, bs_ref, gsum_ref, cw_ref, cb_ref, lng_ref, lnb_ref, dw_ref,
                ya_ref, yc_ref, yd_ref, zc_ref, zd_ref, *, sn):
    g = GROUP_W
    lane = lax.broadcasted_iota(jnp.int32, (1, g), 1)
    n_chunks = sn // CHUNK

    def gmlp_chunk(ci, carry):
        r0 = pl.multiple_of(ci * CHUNK, CHUNK)
        u = jax.nn.gelu(au_ref[0, pl.ds(r0, CHUNK), :].astype(F32))
        v = jax.nn.gelu(av_ref[0, pl.ds(r0, CHUNK), :].astype(F32))
        ss = jnp.dot((v * v).astype(BF16), gsum_ref[...], preferred_element_type=F32)
        vn = (v * lax.rsqrt(ss * (1.0 / HEAD_A) + EPS) * vg_ref[...]).astype(BF16)
        mixed = bs_ref[...]
        for h in range(N_HEADS_A):
            mh = jnp.dot(ws_ref[h], vn, preferred_element_type=F32)
            hm = (lane >= h * HEAD_A) & (lane < (h + 1) * HEAD_A)
            mixed = mixed + jnp.where(hm, mh, 0.0)
        ya_ref[0, pl.ds(r0, CHUNK), :] = (u * mixed).astype(BF16)
        return carry

    lax.fori_loop(0, n_chunks, gmlp_chunk, 0)

    zeros_halo = jnp.zeros((CONV_HALO, g), F32)
    zc_ref[0:CONV_HALO, :] = zeros_halo
    zc_ref[CONV_HALO + sn:CONV_HALO + sn + CONV_HALO, :] = zeros_halo
    zd_ref[0:CONV_HALO, :] = zeros_halo
    zd_ref[CONV_HALO + sn:CONV_HALO + sn + CONV_HALO, :] = zeros_halo

    def stage_chunk(ci, carry):
        r0 = pl.multiple_of(ci * CHUNK, CHUNK)
        ca = ca_ref[0, pl.ds(r0, CHUNK), :].astype(F32)
        cg = cg_ref[0, pl.ds(r0, CHUNK), :].astype(F32)
        zc_ref[pl.ds(r0 + CONV_HALO, CHUNK), :] = ca * jax.nn.sigmoid(cg)
        dc = dc_ref[0, pl.ds(r0, CHUNK), :].astype(F32)
        dh = dh_ref[0, pl.ds(r0, CHUNK), :].astype(F32)
        zd_ref[pl.ds(r0 + CONV_HALO, CHUNK), :] = dc * dh
        return carry

    lax.fori_loop(0, n_chunks, stage_chunk, 0)

    def conv_chunk(ci, carry):
        r0 = pl.multiple_of(ci * CHUNK, CHUNK)
        win = zc_ref[pl.ds(r0, CHUNK + 2 * CONV_HALO), :]
        acc = jnp.zeros((CHUNK, g), F32) + cb_ref[...]
        base = CONV_HALO - CONV_C // 2
        for k in range(CONV_C):
            acc = acc + cw_ref[k:k + 1, :] * win[base + k:base + k + CHUNK, :]
        mu = jnp.mean(acc, axis=-1, keepdims=True)
        dev = acc - mu
        var = jnp.mean(dev * dev, axis=-1, keepdims=True)
        yln = dev * lax.rsqrt(var + EPS) * lng_ref[...] + lnb_ref[...]
        yc_ref[0, pl.ds(r0, CHUNK), :] = (yln * jax.nn.sigmoid(yln)).astype(BF16)

        wind = zd_ref[pl.ds(r0, CHUNK + 2 * CONV_HALO), :]
        accd = jnp.zeros((CHUNK, g), F32)
        based = CONV_HALO - CONV_D // 2
        for k in range(CONV_D):
            accd = accd + dw_ref[k:k + 1, :] * wind[based + k:based + k + CHUNK, :]
        db = db_ref[0, pl.ds(r0, CHUNK), :].astype(F32)
        yd_ref[0, pl.ds(r0, CHUNK), :] = (db * accd).astype(BF16)
        return carry

    lax.fori_loop(0, n_chunks, conv_chunk, 0)


def _mixers(p, vg, ws_bf, bs_exp, gsum64, cw, cb, lng, lnb, dw):
    bn, sn, _ = p.shape
    g = GROUP_W

    def col(j):
        return pl.BlockSpec((1, sn, g), lambda b, j=j: (b, 0, j))

    def full(a):
        nd = a.ndim
        return pl.BlockSpec(a.shape, lambda b, nd=nd: (0,) * nd)

    out_spec = pl.BlockSpec((1, sn, g), lambda b: (b, 0, 0))
    out_sds = jax.ShapeDtypeStruct((bn, sn, g), BF16)
    params = (vg, ws_bf, bs_exp, gsum64, cw, cb, lng, lnb, dw)
    return pl.pallas_call(
        functools.partial(_mix_kernel, sn=sn),
        grid=(bn,),
        in_specs=[col(0), col(1), col(5), col(6), col(7), col(8), col(9)] + [full(a) for a in params],
        out_specs=[out_spec, out_spec, out_spec],
        out_shape=[out_sds, out_sds, out_sds],
        scratch_shapes=[pltpu.VMEM((sn + 2 * CONV_HALO, g), F32),
                        pltpu.VMEM((sn + 2 * CONV_HALO, g), F32)],
        compiler_params=_cparams("parallel"),
        name="mixers",
    )(p, p, p, p, p, p, p, *params)


def _attn_kernel(lam_ref, q_ref, *rest, n_seg):
    k_refs = rest[0:2 * n_seg:2]
    v_refs = rest[1:2 * n_seg:2]
    gpost_ref, gsum_ref, o_ref = rest[2 * n_seg:]
    lam = lam_ref[0]
    q = q_ref[0]
    ks = [r[0] for r in k_refs]
    vs = [r[0] for r in v_refs]
    tq = q.shape[0]
    lane = lax.broadcasted_iota(jnp.int32, (1, GROUP_W), 1)
    y = jnp.zeros((tq, GROUP_W), F32)

    def scores(u):
        lo = u * HEAD_QK_B
        qm = jnp.where((lane >= lo) & (lane < lo + HEAD_QK_B), q, jnp.zeros_like(q))
        return [lax.dot_general(qm, k, (((1,), (1,)), ((), ())), preferred_element_type=F32) for k in ks]

    def softmax_parts(ss):
        mx = jnp.max(ss[0], axis=-1, keepdims=True)
        for s in ss[1:]:
            mx = jnp.maximum(mx, jnp.max(s, axis=-1, keepdims=True))
        es = [jnp.exp2(s - mx) for s in ss]
        den = jnp.sum(es[0], axis=-1, keepdims=True)
        for e in es[1:]:
            den = den + jnp.sum(e, axis=-1, keepdims=True)
        return [e.astype(BF16) for e in es], den

    n_units = 2 * N_HEADS_B
    pending = {u: scores(u) for u in range(min(ATTN_LOOKAHEAD, n_units))}
    parts = {}
    for u in range(n_units):
        if u + ATTN_LOOKAHEAD < n_units:
            pending[u + ATTN_LOOKAHEAD] = scores(u + ATTN_LOOKAHEAD)
        parts[u] = softmax_parts(pending.pop(u))
        if u % 2 == 1:
            h = u // 2
            (num1, den1), (num2, den2) = parts.pop(u - 1), parts.pop(u)
            coef = (lam * den1 / den2).astype(BF16)
            o = jnp.zeros((tq, GROUP_W), F32)
            for e1, e2, v in zip(num1, num2, vs):
                o = o + jnp.dot(e1 - coef * e2, v, preferred_element_type=F32)
            o = o * (1.0 / den1)
            y = jnp.where((lane >= h * HEAD_V_B) & (lane < (h + 1) * HEAD_V_B), o, y)
    ss = jnp.dot((y * y).astype(BF16), gsum_ref[...], preferred_element_type=F32)
    o_ref[0] = (y * lax.rsqrt(ss * (1.0 / HEAD_V_B) + EPS) * gpost_ref[...]).astype(BF16)


def _attention(lam, p_q, kv_sources, gpost, gsum64, tq):
    bn, sq, _ = p_q.shape
    g = GROUP_W
    kv_specs, kv_args = [], []
    for p in kv_sources:
        sk = p.shape[1]
        kv_specs += [pl.BlockSpec((1, sk, g), lambda b, i: (b, 0, 3)),
                     pl.BlockSpec((1, sk, g), lambda b, i: (b, 0, 4))]
        kv_args += [p, p]
    return pl.pallas_call(
        functools.partial(_attn_kernel, n_seg=len(kv_sources)),
        grid=(bn, sq // tq),
        in_specs=[pl.BlockSpec(memory_space=pltpu.SMEM),
                  pl.BlockSpec((1, tq, g), lambda b, i: (b, i, 2))] + kv_specs
                 + [pl.BlockSpec((1, g), lambda b, i: (0, 0)),
                    pl.BlockSpec((g, g), lambda b, i: (0, 0))],
        out_specs=pl.BlockSpec((1, tq, g), lambda b, i: (b, i, 0)),
        out_shape=jax.ShapeDtypeStruct((bn, sq, g), BF16),
        compiler_params=_cparams("parallel", "arbitrary"),
        name="diff_attention",
    )(lam, p_q, *kv_args, gpost, gsum64)


def _out_kernel(ya_ref, yb_ref, yc_ref, yd_ref, x_ref, g1_ref, sc2_ref, sh2_ref, n2g_ref,
                wo_ref, wrh_ref, wrl_ref, br_ref, xo_ref, h2_ref, gate_ref):
    g = GROUP_W
    acc = jnp.dot(ya_ref[0], wo_ref[0:g, :], preferred_element_type=F32)
    acc = acc + jnp.dot(yb_ref[0], wo_ref[g:2 * g, :], preferred_element_type=F32)
    acc = acc + jnp.dot(yc_ref[0], wo_ref[2 * g:3 * g, :], preferred_element_type=F32)
    acc = acc + jnp.dot(yd_ref[0], wo_ref[3 * g:4 * g, :], preferred_element_type=F32)
    xn = x_ref[0] + g1_ref[0] * acc
    xo_ref[0] = xn
    ms = jnp.mean(xn * xn, axis=-1, keepdims=True)
    h2 = (xn * lax.rsqrt(ms + EPS) * n2g_ref[...]) * (1.0 + sc2_ref[0]) + sh2_ref[0]
    hh = h2.astype(BF16)
    h2_ref[0] = hh
    hl = (h2 - hh.astype(F32)).astype(BF16)
    logits = (jnp.dot(hh, wrh_ref[...], preferred_element_type=F32)
              + jnp.dot(hl, wrh_ref[...], preferred_element_type=F32)
              + jnp.dot(hh, wrl_ref[...], preferred_element_type=F32)) + br_ref[...]
    tm = logits.shape[0]
    lane = lax.broadcasted_iota(jnp.int32, (tm, GATE_LANES), 1)
    neg_inf = jnp.float32(-jnp.inf)
    l = jnp.where(lane < N_EXPERTS, logits, neg_inf)
    vals, sels = [], []
    for _ in range(TOP_K):
        m = jnp.max(l, axis=-1, keepdims=True)
        idx = jnp.min(jnp.where(l == m, lane, GATE_LANES), axis=-1, keepdims=True)
        sel = lane == idx
        vals.append(m)
        sels.append(sel)
        l = jnp.where(sel, neg_inf, l)
    es = [jnp.exp(vk - vals[0]) for vk in vals]
    inv = 1.0 / (es[0] + es[1] + es[2] + es[3])
    gates = jnp.zeros((tm, GATE_LANES), F32)
    for sel, ek in zip(sels, es):
        gates = jnp.where(sel, ek * inv, gates)
    gate_ref[0] = gates


def _out_proj(ya, yb, yc, yd, x, g1, sc2, sh2, n2g, wo_bf, wr_hi, wr_lo, br, tm):
    bn, sn, d = x.shape
    g = GROUP_W
    yspec = pl.BlockSpec((1, tm, g), lambda b, s: (b, s, 0))
    vec_spec = pl.BlockSpec((1, 1, d), lambda b, s: (b, 0, 0))
    xspec = pl.BlockSpec((1, tm, d), lambda b, s: (b, s, 0))
    return pl.pallas_call(
        _out_kernel,
        grid=(bn, sn // tm),
        in_specs=[yspec, yspec, yspec, yspec, xspec, vec_spec, vec_spec, vec_spec,
                  pl.BlockSpec((1, d), lambda b, s: (0, 0)),
                  pl.BlockSpec((d, d), lambda b, s: (0, 0)),
                  pl.BlockSpec((d, GATE_LANES), lambda b, s: (0, 0)),
                  pl.BlockSpec((d, GATE_LANES), lambda b, s: (0, 0)),
                  pl.BlockSpec((1, GATE_LANES), lambda b, s: (0, 0))],
        out_specs=[xspec, xspec, pl.BlockSpec((1, tm, GATE_LANES), lambda b, s: (b, s, 0))],
        out_shape=[jax.ShapeDtypeStruct((bn, sn, d), F32),
                   jax.ShapeDtypeStruct((bn, sn, d), BF16),
                   jax.ShapeDtypeStruct((bn, sn, GATE_LANES), F32)],
        compiler_params=_cparams("parallel", "arbitrary"),
        name="out_proj",
    )(ya, yb, yc, yd, x, g1, sc2, sh2, n2g, wo_bf, wr_hi, wr_lo, br)


def _mask_bf16(m):
    return jnp.where(m, 1.0, 0.0).astype(BF16)


def _dispatch_kernel(tab_ref, h2_ref, g_ref, *rest):
    xs_hbm, buf, sem, cnt_s = rest[-4:]
    b = pl.program_id(0)
    nb = pl.num_programs(0)
    slot = b % 2
    ch = CHUNK_ROWS
    tb = TOKEN_BLOCK
    gl = GATE_LANES

    def chunk_copy(s, src, dst):
        return pltpu.make_async_copy(buf.at[s, pl.ds(src, ch), :], xs_hbm.at[pl.ds(dst, ch), :], sem.at[s])

    def wait_chunks(n, s):
        def body(c, carry):
            chunk_copy(s, 0, 0).wait()
            return carry
        lax.fori_loop(0, n, body, 0)

    @pl.when(b >= 2)
    def _():
        wait_chunks(cnt_s[slot], slot)

    mt = g_ref[...].T > 0.0
    mt_bf = _mask_bf16(mt)
    ti = lax.broadcasted_iota(jnp.int32, (tb, tb), 0)
    tj = lax.broadcasted_iota(jnp.int32, (tb, tb), 1)
    rank = jnp.dot(mt_bf, _mask_bf16(ti < tj), preferred_element_type=F32)
    cnt = jnp.sum(jnp.where(mt, 1.0, 0.0), axis=1, keepdims=True)
    nch = jnp.floor((cnt + (ch - 1.0)) * (1.0 / ch))
    ei = lax.broadcasted_iota(jnp.int32, (gl, gl), 0)
    ej = lax.broadcasted_iota(jnp.int32, (gl, gl), 1)
    loc = ch * jnp.dot(_mask_bf16(ej < ei), jnp.broadcast_to(nch, (gl, gl)).astype(BF16),
                       preferred_element_type=F32)[:, 0:1]
    order = jnp.dot(_mask_bf16(ej <= ei), mt_bf, preferred_element_type=F32)
    dest = loc + rank
    dks = []
    for k in range(TOP_K):
        selk = mt & (order == float(k + 1))
        dk = jnp.sum(jnp.where(selk, dest + 1.0, 0.0), axis=0, keepdims=True) - 1.0
        dks.append(dk.astype(jnp.int32))
    h2 = h2_ref[...]
    for rg in range(LOCAL_ROWS // ROW_GROUP):
        ri = lax.broadcasted_iota(jnp.int32, (ROW_GROUP, tb), 0) + rg * ROW_GROUP
        hit = (ri == dks[0]) | (ri == dks[1]) | (ri == dks[2]) | (ri == dks[3])
        srt = jnp.dot(_mask_bf16(hit), h2, preferred_element_type=F32)
        buf[slot, rg * ROW_GROUP:(rg + 1) * ROW_GROUP, :] = srt.astype(BF16)

    n = tab_ref[0, 0, TABLE_W - 1]
    cnt_s[slot] = n

    def issue(c, carry):
        dst = pl.multiple_of(tab_ref[0, 0, c] * ch, ch)
        chunk_copy(slot, pl.multiple_of(c * ch, ch), dst).start()
        return carry
    lax.fori_loop(0, n, issue, 0)

    @pl.when(b == nb - 1)
    def _():
        wait_chunks(n, slot)

        @pl.when(b >= 1)
        def _():
            wait_chunks(cnt_s[1 - slot], 1 - slot)


def _dispatch(table, blk0, h2, gates, rtot, xs_prev=None):
    n, d = h2.shape
    nb = n // TOKEN_BLOCK
    in_specs = [pl.BlockSpec((1, 1, TABLE_W), lambda b: (b + blk0, 0, 0), memory_space=pltpu.SMEM),
                pl.BlockSpec((TOKEN_BLOCK, d), lambda b: (b, 0)),
                pl.BlockSpec((TOKEN_BLOCK, GATE_LANES), lambda b: (b, 0))]
    args = [table, h2, gates]
    aliases = {}
    if xs_prev is not None:
        in_specs.append(pl.BlockSpec(memory_space=pl.ANY))
        args.append(xs_prev)
        aliases = {3: 0}
    return pl.pallas_call(
        _dispatch_kernel,
        grid=(nb,),
        in_specs=in_specs,
        out_specs=pl.BlockSpec(memory_space=pl.ANY),
        out_shape=jax.ShapeDtypeStruct((rtot, d), BF16),
        scratch_shapes=[pltpu.VMEM((2, LOCAL_ROWS, d), BF16),
                        pltpu.SemaphoreType.DMA((2,)),
                        pltpu.SMEM((2,), jnp.int32)],
        input_output_aliases=aliases,
        compiler_params=_cparams("arbitrary"),
        name="moe_dispatch",
    )(*args)


def _combine_kernel(tab_ref, x_ref, g2_ref, g_ref, y_hbm, o_ref, ybuf, sem):
    ch = CHUNK_ROWS
    tb = TOKEN_BLOCK
    gl = GATE_LANES
    n = tab_ref[0, 0, TABLE_W - 1]

    def chunk_copy(src, dst):
        return pltpu.make_async_copy(y_hbm.at[pl.ds(src, ch), :], ybuf.at[pl.ds(dst, ch), :], sem)

    def issue(c, carry):
        src = pl.multiple_of(tab_ref[0, 0, c] * ch, ch)
        chunk_copy(src, pl.multiple_of(c * ch, ch)).start()
        return carry
    lax.fori_loop(0, n, issue, 0)

    g = g_ref[...]
    m = g > 0.0
    m_bf = _mask_bf16(m)
    ti = lax.broadcasted_iota(jnp.int32, (tb, tb), 0)
    tj = lax.broadcasted_iota(jnp.int32, (tb, tb), 1)
    rank = jnp.dot(_mask_bf16(tj < ti), m_bf, preferred_element_type=F32)
    cnt = jnp.sum(jnp.where(m, 1.0, 0.0), axis=0, keepdims=True)
    nch = jnp.floor((cnt + (ch - 1.0)) * (1.0 / ch))
    ei = lax.broadcasted_iota(jnp.int32, (gl, gl), 0)
    ej = lax.broadcasted_iota(jnp.int32, (gl, gl), 1)
    loc = ch * jnp.dot(jnp.broadcast_to(nch, (8, gl)).astype(BF16), _mask_bf16(ei < ej),
                       preferred_element_type=F32)[0:1, :]
    order = jnp.dot(m_bf, _mask_bf16(ei <= ej), preferred_element_type=F32)
    dest = loc + rank
    dks, gks = [], []
    for k in range(TOP_K):
        selk = m & (order == float(k + 1))
        dk = jnp.sum(jnp.where(selk, dest + 1.0, 0.0), axis=1, keepdims=True) - 1.0
        dks.append(dk.astype(jnp.int32))
        gks.append(jnp.sum(jnp.where(selk, g, 0.0), axis=1, keepdims=True))

    def wait_one(c, carry):
        chunk_copy(0, 0).wait()
        return carry
    lax.fori_loop(0, n, wait_one, 0)

    acc = jnp.zeros(x_ref.shape, F32)
    for rg in range(LOCAL_ROWS // ROW_GROUP):
        li = lax.broadcasted_iota(jnp.int32, (tb, ROW_GROUP), 1) + rg * ROW_GROUP
        cmat = jnp.zeros((tb, ROW_GROUP), F32)
        for dk, gk in zip(dks, gks):
            cmat = cmat + jnp.where(li == dk, gk, 0.0)
        ri = lax.broadcasted_iota(jnp.int32, (ROW_GROUP, 1), 0) + rg * ROW_GROUP
        yv = ybuf[rg * ROW_GROUP:(rg + 1) * ROW_GROUP, :]
        yv = jnp.where(ri < n * ch, yv, jnp.zeros_like(yv))
        acc = acc + jnp.dot(cmat.astype(BF16), yv, preferred_element_type=F32)
    o_ref[...] = x_ref[...] + g2_ref[0] * acc


def _combine(table, blk0, x, g2, gates, ys):
    bn, sn, d = x.shape
    n = bn * sn
    per_seq = sn // TOKEN_BLOCK
    out = pl.pallas_call(
        _combine_kernel,
        grid=(n // TOKEN_BLOCK,),
        in_specs=[pl.BlockSpec((1, 1, TABLE_W), lambda b: (b + blk0, 0, 0), memory_space=pltpu.SMEM),
                  pl.BlockSpec((TOKEN_BLOCK, d), lambda b: (b, 0)),
                  pl.BlockSpec((1, 1, d), lambda b: (b // per_seq, 0, 0)),
                  pl.BlockSpec((TOKEN_BLOCK, GATE_LANES), lambda b: (b, 0)),
                  pl.BlockSpec(memory_space=pl.ANY)],
        out_specs=pl.BlockSpec((TOKEN_BLOCK, d), lambda b: (b, 0)),
        out_shape=jax.ShapeDtypeStruct((n, d), F32),
        scratch_shapes=[pltpu.VMEM((LOCAL_ROWS, d), BF16),
                        pltpu.SemaphoreType.DMA(())],
        compiler_params=_cparams("arbitrary"),
        name="moe_combine",
    )(table, x.reshape(n, d), g2, gates.reshape(n, GATE_LANES), ys)
    return out.reshape(bn, sn, d)


def _expert_kernel(te_ref, tf_ref, nv_ref, x_ref, wgu_ref, bgu_ref, wdn_ref, bdn_ref, y_ref,
                   wgu_s, wdn_s):
    i = pl.program_id(0)
    flag = tf_ref[i]
    d, f2 = wgu_s.shape
    f = f2 // 2
    rows = 128

    @pl.when((flag & 2) != 0)
    def _():
        def cast_gu(r, c):
            r0 = pl.multiple_of(r * rows, rows)
            wgu_s[pl.ds(r0, rows), :] = wgu_ref[0, pl.ds(r0, rows), :].astype(BF16)
            return c
        lax.fori_loop(0, d // rows, cast_gu, 0)

        def cast_dn(r, c):
            r0 = pl.multiple_of(r * rows, rows)
            wdn_s[pl.ds(r0, rows), :] = wdn_ref[0, pl.ds(r0, rows), :].astype(BF16)
            return c
        lax.fori_loop(0, f // rows, cast_dn, 0)

    @pl.when((flag & 1) != 0)
    def _():
        ri = lax.broadcasted_iota(jnp.int32, (x_ref.shape[0], 1), 0)
        x = x_ref[...]
        x = jnp.where(ri < nv_ref[i], x, jnp.zeros_like(x))
        gu = jnp.dot(x, wgu_s[...], preferred_element_type=F32) + bgu_ref[0]
        gl = jnp.minimum(gu[:, :f], SWIGLU_LIMIT)
        up = jnp.clip(gu[:, f:], -SWIGLU_LIMIT, SWIGLU_LIMIT)
        hid = (up + 1.0) * gl * jax.nn.sigmoid(SWIGLU_ALPHA * gl)
        y = jnp.dot(hid.astype(BF16), wdn_s[...], preferred_element_type=F32) + bdn_ref[0]
        y_ref[...] = y.astype(BF16)

    @pl.when((flag & 1) == 0)
    def _():
        y_ref[...] = jnp.zeros_like(y_ref)


def _experts(tile_expert, tile_flag, tile_rows, xs, w_gu, b_gu, w_dn, b_dn, tm):
    rtot, d = xs.shape
    ne, _, f2 = w_gu.shape
    f = f2 // 2
    grid_spec = pltpu.PrefetchScalarGridSpec(
        num_scalar_prefetch=3,
        grid=(rtot // tm,),
        in_specs=[pl.BlockSpec((tm, d), lambda i, te, tf, nv: (i, 0)),
                  pl.BlockSpec((1, d, f2), lambda i, te, tf, nv: (te[i], 0, 0)),
                  pl.BlockSpec((1, 1, f2), lambda i, te, tf, nv: (te[i], 0, 0)),
                  pl.BlockSpec((1, f, d), lambda i, te, tf, nv: (te[i], 0, 0)),
                  pl.BlockSpec((1, 1, d), lambda i, te, tf, nv: (te[i], 0, 0))],
        out_specs=pl.BlockSpec((tm, d), lambda i, te, tf, nv: (i, 0)),
        scratch_shapes=[pltpu.VMEM((d, f2), BF16), pltpu.VMEM((f, d), BF16)],
    )
    return pl.pallas_call(
        _expert_kernel,
        grid_spec=grid_spec,
        out_shape=jax.ShapeDtypeStruct((rtot, d), BF16),
        compiler_params=_cparams("arbitrary"),
        name="experts",
    )(tile_expert, tile_flag, tile_rows, xs, w_gu, b_gu.reshape(ne, 1, f2), w_dn, b_dn.reshape(ne, 1, d))


def _moe_layout(gate_streams, tm):
    n = sum(gs.shape[0] for gs in gate_streams)
    nb = n // TOKEN_BLOCK
    ne = N_EXPERTS
    ch = CHUNK_ROWS
    tmc = tm // ch
    n_tiles = -(-(n * TOP_K + nb * ne * (ch - 1) + ne * (tm - ch)) // tm)
    i32 = jnp.int32
    cnt = jnp.concatenate(
        [jnp.sum((gs[:, :ne] > 0.0).astype(i32).reshape(-1, TOKEN_BLOCK, ne), axis=1) for gs in gate_streams], axis=0)
    nch = (cnt + ch - 1) // ch
    loc_end = jnp.cumsum(nch, axis=1)
    loc_off = loc_end - nch
    exp_ch = jnp.sum(nch, axis=0)
    exp_pad = ((exp_ch + tmc - 1) // tmc) * tmc
    exp_end = jnp.cumsum(exp_pad)
    exp_off = exp_end - exp_pad
    seg_off = exp_off[None, :] + jnp.cumsum(nch, axis=0) - nch
    c = jnp.arange(MAX_CHUNKS, dtype=i32)
    e_of_c = jnp.minimum(jnp.sum((loc_end[:, None, :] <= c[None, :, None]).astype(i32), axis=2), ne - 1)
    onehot = (e_of_c[:, :, None] == jnp.arange(ne, dtype=i32)).astype(i32)
    gdst = jnp.sum(onehot * (seg_off - loc_off)[:, None, :], axis=2) + c[None, :]
    nblk = loc_end[:, ne - 1]
    gdst = jnp.where(c[None, :] < nblk[:, None], gdst, 0)
    table = jnp.concatenate([gdst, jnp.zeros((nb, TABLE_W - MAX_CHUNKS - 1), i32), nblk[:, None]], axis=1)
    tile_start = jnp.arange(n_tiles, dtype=i32) * tmc
    te = jnp.minimum(jnp.sum((exp_end[None, :] <= tile_start[:, None]).astype(i32), axis=1), ne - 1)
    te_hot = (te[:, None] == jnp.arange(ne, dtype=i32)).astype(i32)
    t_off = jnp.sum(te_hot * exp_off[None, :], axis=1)
    t_len = jnp.sum(te_hot * exp_ch[None, :], axis=1)
    total = exp_end[ne - 1]
    active = tile_start < total
    first = active & (tile_start == t_off)
    tile_rows = jnp.clip((t_len - (tile_start - t_off)) * ch, 0, tm)
    last_active = (tile_start + tmc == total).astype(i32)
    te = jnp.where(active, te, jnp.sum(last_active * te))
    tile_flag = active.astype(i32) + 2 * first.astype(i32)
    return table.reshape(nb, 1, TABLE_W), te, tile_flag, tile_rows, n_tiles * tm


def _rope_tables(sn, scale_q, roped):
    if not roped:
        ones = jnp.ones((sn, GROUP_W), F32)
        zeros = jnp.zeros((sn, GROUP_W), F32)
        return ones * scale_q, zeros, ones, zeros
    pos = jnp.arange(sn)
    row = (pos // GRID_W).astype(F32)
    col = (pos % GRID_W).astype(F32)
    nf = HEAD_QK_B // 4
    inv = ROPE_BASE ** (-jnp.arange(nf, dtype=F32) / nf)
    ar = row[:, None] * inv
    ac = col[:, None] * inv
    cos32 = jnp.concatenate([jnp.cos(ar), jnp.cos(ar), jnp.cos(ac), jnp.cos(ac)], axis=1)
    sin32 = jnp.concatenate([-jnp.sin(ar), jnp.sin(ar), -jnp.sin(ac), jnp.sin(ac)], axis=1)
    reps = GROUP_W // HEAD_QK_B
    cos = jnp.tile(cos32, (1, reps))
    sin = jnp.tile(sin32, (1, reps))
    return cos * scale_q, sin * scale_q, cos, sin


def kernel(x, c, ctx, c_ctx, w_mod, b_mod, norm1_g, norm2_g, w_in, a_vnorm_g, a_ws, a_bs, b_qnorm_g, b_knorm_g, b_lam_q1, b_lam_k1, b_lam_q2, b_lam_k2, b_subln_g, c_conv_w, c_conv_b, c_ln_g, c_ln_b, d_conv_w, w_out, router_w, router_b, exp_w_gu, exp_b_gu, exp_w_dn, exp_b_dn):
    bsz, seq, dm = x.shape
    ctx_len = ctx.shape[1]
    depth = w_in.shape[0]
    g = GROUP_W
    tm_lat = min(512, seq)
    tm_ctx = min(512, ctx_len)
    tq_lat = min(512, seq)
    tq_ctx = min(256, ctx_len)
    tm_moe = 512
    assert seq % TOKEN_BLOCK == 0 and ctx_len % TOKEN_BLOCK == 0

    n_mod_rows = -(-(bsz + 1) // 8) * 8
    cc = jnp.concatenate([c, c_ctx[None, :], jnp.zeros((n_mod_rows - bsz - 1, dm), F32)], axis=0)
    mods = _modulation(cc, w_mod, b_mod)

    gsum32 = _group_ones(g, HEAD_QK_B)
    gsum64 = _group_ones(g, HEAD_V_B)
    scale_q = HEAD_QK_B ** -0.5 * math.log2(math.e)
    tabs_lat = _rope_tables(seq, scale_q, True)
    tabs_ctx = _rope_tables(ctx_len, scale_q, False)

    xc = ctx
    for l in range(depth):
        last = l == depth - 1
        lam_init = 0.8 - 0.6 * math.exp(-0.3 * l)
        lam = (jnp.exp(jnp.sum(b_lam_q1[l] * b_lam_k1[l])) - jnp.exp(jnp.sum(b_lam_q2[l] * b_lam_k2[l]))
               + lam_init).astype(F32).reshape(1)
        m_lat = mods[l, :bsz].reshape(bsz, 6, 1, dm)
        m_ctx = jnp.broadcast_to(mods[l, bsz].reshape(1, 6, 1, dm), (bsz, 6, 1, dm))
        sh1, sc1, g1, sh2, sc2, g2 = [m_lat[:, i] for i in range(6)]
        sh1c, sc1c, g1c, sh2c, sc2c, g2c = [m_ctx[:, i] for i in range(6)]

        w_in_bf = w_in[l].astype(BF16)
        n1g = norm1_g[l][None, :]
        n2g = norm2_g[l][None, :]
        reps = g // HEAD_QK_B // 2
        qg = jnp.tile(b_qnorm_g[l].reshape(1, 2 * HEAD_QK_B), (1, reps))
        kg = jnp.tile(b_knorm_g[l].reshape(1, 2 * HEAD_QK_B), (1, reps))
        gpost = jnp.tile(b_subln_g[l][None, :], (1, N_HEADS_B)) * (1.0 - lam_init)
        vg = a_vnorm_g[l][None, :]
        ws_bf = a_ws[l].astype(BF16)
        bs_exp = jnp.repeat(a_bs[l].T, HEAD_A, axis=1)
        mix_params = (vg, ws_bf, bs_exp, gsum64, c_conv_w[l], c_conv_b[l][None, :], c_ln_g[l][None, :],
                      c_ln_b[l][None, :], d_conv_w[l])
        wo_bf = w_out[l].astype(BF16)
        wr = jnp.pad(router_w[l], ((0, 0), (0, GATE_LANES - N_EXPERTS)))
        wr_hi = wr.astype(BF16)
        wr_lo = (wr - wr_hi.astype(F32)).astype(BF16)
        br = jnp.pad(router_b[l], (0, GATE_LANES - N_EXPERTS))[None, :]

        p_lat = _in_proj(x, sc1, sh1, n1g, w_in_bf, gsum32, qg, kg, tabs_lat, tm_lat)
        p_ctx = _in_proj(xc, sc1c, sh1c, n1g, w_in_bf, gsum32, qg, kg, tabs_ctx, tm_ctx)

        yb = _attention(lam, p_lat, (p_lat, p_ctx), gpost, gsum64, tq_lat)
        ya, yc, yd = _mixers(p_lat, *mix_params)
        x, h2, gates = _out_proj(ya, yb, yc, yd, x, g1, sc2, sh2, n2g, wo_bf, wr_hi, wr_lo, br, tm_lat)
        moe_w = (exp_w_gu[l], exp_b_gu[l], exp_w_dn[l], exp_b_dn[l])
        h2f = h2.reshape(-1, dm)
        gatesf = gates.reshape(-1, GATE_LANES)
        if last:
            table, te, tf, nv, rtot = _moe_layout((gatesf,), tm_moe)
            xs = _dispatch(table, 0, h2f, gatesf, rtot)
            ys = _experts(te, tf, nv, xs, *moe_w, tm_moe)
            x = _combine(table, 0, x, g2, gates, ys)
        else:
            ybc = _attention(lam, p_ctx, (p_ctx,), gpost, gsum64, tq_ctx)
            yac, ycc, ydc = _mixers(p_ctx, *mix_params)
            xc, h2c, gatesc = _out_proj(yac, ybc, ycc, ydc, xc, g1c, sc2c, sh2c, n2g, wo_bf, wr_hi, wr_lo, br,
                                        tm_ctx)
            gatescf = gatesc.reshape(-1, GATE_LANES)
            blk_ctx = bsz * seq // TOKEN_BLOCK
            table, te, tf, nv, rtot = _moe_layout((gatesf, gatescf), tm_moe)
            xs = _dispatch(table, 0, h2f, gatesf, rtot)
            xs = _dispatch(table, blk_ctx, h2c.reshape(-1, dm), gatescf, rtot, xs_prev=xs)
            ys = _experts(te, tf, nv, xs, *moe_w, tm_moe)
            x = _combine(table, 0, x, g2, gates, ys)
            xc = _combine(table, blk_ctx, xc, g2c, gatesc, ys)
    return x
```

```python
import functools
import math

import jax
import jax.numpy as jnp
from jax import lax
from jax.experimental import pallas as pl
from jax.experimental.pallas import tpu as pltpu

F32 = jnp.float32
BF16 = jnp.bfloat16

GROUP_W = 256
N_SPLITS = 10
N_HEADS_A = 4
HEAD_A = 64
CHUNK = 128
N_HEADS_B = 4
HEAD_V_B = 64
HEAD_QK_B = 32
GRID_W = 64
ROPE_BASE = 10000.0
CONV_C = 31
CONV_D = 3
N_EXPERTS = 32
TOP_K = 4
SWIGLU_LIMIT = 7.0
SWIGLU_ALPHA = 1.702
EPS = 1e-6

GATE_LANES = 128
CONV_HALO = 16
CONV_SPAN = CHUNK + 8 * ((CONV_HALO - CONV_C // 2 + CONV_C - 1) // 8)
VMEM_LIMIT = 56 * 1024 * 1024
MAX_FIXED_SHIFT = 60.0
ATTN_LOOKAHEAD = 2

TOKEN_BLOCK = 256
CHUNK_ROWS = 16
ROW_GROUP = 256
LOCAL_ROWS = -(-(TOKEN_BLOCK * TOP_K + N_EXPERTS * (CHUNK_ROWS - 1)) // ROW_GROUP) * ROW_GROUP
MAX_CHUNKS = LOCAL_ROWS // CHUNK_ROWS
TABLE_W = 128


def _cparams(*sem):
    return pltpu.CompilerParams(dimension_semantics=sem, vmem_limit_bytes=VMEM_LIMIT)


def _group_ones(width, group):
    i = jnp.arange(width) // group
    return (i[:, None] == i[None, :]).astype(BF16)


def _mod_kernel(c_ref, w_ref, b_ref, o_ref):
    c = c_ref[...]
    a = (c * jax.nn.sigmoid(c)).astype(BF16)
    o_ref[0] = jnp.dot(a, w_ref[0].astype(BF16), preferred_element_type=F32) + b_ref[0]


def _modulation(cc, w_mod, b_mod):
    nl, d, d6 = w_mod.shape
    r = cc.shape[0]
    return pl.pallas_call(
        _mod_kernel,
        grid=(nl, d6 // d),
        in_specs=[pl.BlockSpec((r, d), lambda l, j: (0, 0)),
                  pl.BlockSpec((1, d, d), lambda l, j: (l, 0, j)),
                  pl.BlockSpec((1, 1, d), lambda l, j: (l, 0, j))],
        out_specs=pl.BlockSpec((1, r, d), lambda l, j: (l, 0, j)),
        out_shape=jax.ShapeDtypeStruct((nl, r, d6), F32),
        compiler_params=_cparams("arbitrary", "arbitrary"),
        name="modulation",
    )(cc, w_mod, b_mod.reshape(nl, 1, d6))


def _rope_norm(t, gsum, gain, cos, sin, lo_lane):
    ss = jnp.dot((t * t).astype(BF16), gsum, preferred_element_type=F32)
    tn = t * lax.rsqrt(ss * (1.0 / HEAD_QK_B) + EPS) * gain
    rot = jnp.where(lo_lane, pltpu.roll(tn, GROUP_W - 8, 1), pltpu.roll(tn, 8, 1))
    return tn * cos + rot * sin


def _in_kernel(x_ref, sc_ref, sh_ref, g_ref, w_ref, gsum_ref, qg_ref, kg_ref,
               cq_ref, sq_ref, ck_ref, sk_ref, o_ref):
    x = x_ref[0]
    ms = jnp.mean(x * x, axis=-1, keepdims=True)
    h = (x * lax.rsqrt(ms + EPS) * g_ref[...]) * (1.0 + sc_ref[0]) + sh_ref[0]
    p = jnp.dot(h.astype(BF16), w_ref[...], preferred_element_type=F32)
    g = GROUP_W
    lane = lax.broadcasted_iota(jnp.int32, (1, g), 1)
    lo_lane = (lane & 8) == 0
    gsum = gsum_ref[...]
    q = _rope_norm(p[:, 2 * g:3 * g], gsum, qg_ref[...], cq_ref[...], sq_ref[...], lo_lane)
    k = _rope_norm(p[:, 3 * g:4 * g], gsum, kg_ref[...], ck_ref[...], sk_ref[...], lo_lane)
    o_ref[0, :, 0:2 * g] = p[:, 0:2 * g].astype(BF16)
    o_ref[0, :, 2 * g:3 * g] = q.astype(BF16)
    o_ref[0, :, 3 * g:4 * g] = k.astype(BF16)
    o_ref[0, :, 4 * g:] = p[:, 4 * g:].astype(BF16)


def _in_proj(x, sc, sh, g, w_bf, gsum32, qg, kg, tabs, tm):
    bn, sn, d = x.shape
    d_in = w_bf.shape[1]
    tab_spec = pl.BlockSpec((tm, GROUP_W), lambda b, s: (s, 0))
    vec_spec = pl.BlockSpec((1, 1, d), lambda b, s: (b, 0, 0))
    row256 = pl.BlockSpec((1, GROUP_W), lambda b, s: (0, 0))
    return pl.pallas_call(
        _in_kernel,
        grid=(bn, sn // tm),
        in_specs=[pl.BlockSpec((1, tm, d), lambda b, s: (b, s, 0)),
                  vec_spec, vec_spec,
                  pl.BlockSpec((1, d), lambda b, s: (0, 0)),
                  pl.BlockSpec((d, d_in), lambda b, s: (0, 0)),
                  pl.BlockSpec((GROUP_W, GROUP_W), lambda b, s: (0, 0)),
                  row256, row256, tab_spec, tab_spec, tab_spec, tab_spec],
        out_specs=pl.BlockSpec((1, tm, d_in), lambda b, s: (b, s, 0)),
        out_shape=jax.ShapeDtypeStruct((bn, sn, d_in), BF16),
        compiler_params=_cparams("parallel", "arbitrary"),
        name="in_proj",
    )(x, sc, sh, g, w_bf, gsum32, qg, kg, *tabs)


def _mix_kernel(au_ref, av_ref, ca_ref, cg_ref, db_ref, dc_ref, dh_ref,
                vg_ref, ws_ref, bs_ref, gsum_ref, cw_ref, cb_ref, lng_ref, lnb_ref, dw_ref,
                ya_ref, yc_ref, yd_ref, zc_ref, zd_ref, zs_ref, *, sn):
    g = GROUP_W
    lane = lax.broadcasted_iota(jnp.int32, (1, g), 1)
    n_chunks = sn // CHUNK

    def gmlp_chunk(ci, carry):
        r0 = pl.multiple_of(ci * CHUNK, CHUNK)
        u = jax.nn.gelu(au_ref[0, pl.ds(r0, CHUNK), :].astype(F32))
        v = jax.nn.gelu(av_ref[0, pl.ds(r0, CHUNK), :].astype(F32))
        ss = jnp.dot((v * v).astype(BF16), gsum_ref[...], preferred_element_type=F32)
        vn = (v * lax.rsqrt(ss * (1.0 / HEAD_A) + EPS) * vg_ref[...]).astype(BF16)
        mixed = bs_ref[...]
        for h in range(N_HEADS_A):
            mh = jnp.dot(ws_ref[h], vn, preferred_element_type=F32)
            hm = (lane >= h * HEAD_A) & (lane < (h + 1) * HEAD_A)
            mixed = mixed + jnp.where(hm, mh, 0.0)
        ya_ref[0, pl.ds(r0, CHUNK), :] = (u * mixed).astype(BF16)
        return carry

    lax.fori_loop(0, n_chunks, gmlp_chunk, 0)

    zeros_halo = jnp.zeros((CONV_HALO, g), F32)
    zc_ref[0:CONV_HALO, :] = zeros_halo
    zc_ref[CONV_HALO + sn:CONV_HALO + sn + CONV_HALO, :] = zeros_halo
    zd_ref[0:CONV_HALO, :] = zeros_halo
    zd_ref[CONV_HALO + sn:CONV_HALO + sn + CONV_HALO, :] = zeros_halo

    def stage_chunk(ci, carry):
        r0 = pl.multiple_of(ci * CHUNK, CHUNK)
        ca = ca_ref[0, pl.ds(r0, CHUNK), :].astype(F32)
        cg = cg_ref[0, pl.ds(r0, CHUNK), :].astype(F32)
        zc_ref[pl.ds(r0 + CONV_HALO, CHUNK), :] = ca * jax.nn.sigmoid(cg)
        dc = dc_ref[0, pl.ds(r0, CHUNK), :].astype(F32)
        dh = dh_ref[0, pl.ds(r0, CHUNK), :].astype(F32)
        zd_ref[pl.ds(r0 + CONV_HALO, CHUNK), :] = dc * dh
        return carry

    lax.fori_loop(0, n_chunks, stage_chunk, 0)

    def conv_chunk(ci, carry):
        r0 = pl.multiple_of(ci * CHUNK, CHUNK)
        win = zc_ref[pl.ds(r0, CHUNK + 2 * CONV_HALO), :]
        acc = jnp.zeros((CHUNK, g), F32) + cb_ref[...]
        base = CONV_HALO - CONV_C // 2
        for ph in range(8):
            taps = [k for k in range(CONV_C) if (base + k) % 8 == ph]
            if taps:
                zs_ref[ph] = win[ph:ph + CONV_SPAN, :]
                for k in taps:
                    a = (base + k) // 8 * 8
                    acc = acc + cw_ref[k:k + 1, :] * zs_ref[ph, a:a + CHUNK, :]
        mu = jnp.mean(acc, axis=-1, keepdims=True)
        dev = acc - mu
        var = jnp.mean(dev * dev, axis=-1, keepdims=True)
        yln = dev * lax.rsqrt(var + EPS) * lng_ref[...] + lnb_ref[...]
        yc_ref[0, pl.ds(r0, CHUNK), :] = (yln * jax.nn.sigmoid(yln)).astype(BF16)

        wind = zd_ref[pl.ds(r0, CHUNK + 2 * CONV_HALO), :]
        accd = jnp.zeros((CHUNK, g), F32)
        based = CONV_HALO - CONV_D // 2
        for k in range(CONV_D):
            accd = accd + dw_ref[k:k + 1, :] * wind[based + k:based + k + CHUNK, :]
        db = db_ref[0, pl.ds(r0, CHUNK), :].astype(F32)
        yd_ref[0, pl.ds(r0, CHUNK), :] = (db * accd).astype(BF16)
        return carry

    lax.fori_loop(0, n_chunks, conv_chunk, 0)


def _mixers(p, vg, ws_bf, bs_exp, gsum64, cw, cb, lng, lnb, dw):
    bn, sn, _ = p.shape
    g = GROUP_W

    def col(j):
        return pl.BlockSpec((1, sn, g), lambda b, j=j: (b, 0, j))

    def full(a):
        nd = a.ndim
        return pl.BlockSpec(a.shape, lambda b, nd=nd: (0,) * nd)

    out_spec = pl.BlockSpec((1, sn, g), lambda b: (b, 0, 0))
    out_sds = jax.ShapeDtypeStruct((bn, sn, g), BF16)
    params = (vg, ws_bf, bs_exp, gsum64, cw, cb, lng, lnb, dw)
    return pl.pallas_call(
        functools.partial(_mix_kernel, sn=sn),
        grid=(bn,),
        in_specs=[col(0), col(1), col(5), col(6), col(7), col(8), col(9)] + [full(a) for a in params],
        out_specs=[out_spec, out_spec, out_spec],
        out_shape=[out_sds, out_sds, out_sds],
        scratch_shapes=[pltpu.VMEM((sn + 2 * CONV_HALO, g), F32),
                        pltpu.VMEM((sn + 2 * CONV_HALO, g), F32),
                        pltpu.VMEM((8, CONV_SPAN, g), F32)],
        compiler_params=_cparams("parallel"),
        name="mixers",
    )(p, p, p, p, p, p, p, *params)


def _attn_kernel(lam_ref, q_ref, *rest, n_seg):
    k_refs = rest[0:2 * n_seg:2]
    v_refs = rest[1:2 * n_seg:2]
    gpost_ref, gsum_ref, o_ref = rest[2 * n_seg:]
    lam = lam_ref[0]
    score_bound = lam_ref[1]
    q = q_ref[0]
    ks = [r[0] for r in k_refs]
    vs = [r[0] for r in v_refs]
    tq = q.shape[0]
    lane = lax.broadcasted_iota(jnp.int32, (1, GROUP_W), 1)

    def scores(u):
        lo = u * HEAD_QK_B
        qm = jnp.where((lane >= lo) & (lane < lo + HEAD_QK_B), q, jnp.zeros_like(q))
        return [lax.dot_general(qm, k, (((1,), (1,)), ((), ())), preferred_element_type=F32) for k in ks]

    def softmax_parts(ss, bounded):
        if bounded:
            mx = score_bound
        else:
            mx = jnp.max(ss[0], axis=-1, keepdims=True)
            for s in ss[1:]:
                mx = jnp.maximum(mx, jnp.max(s, axis=-1, keepdims=True))
        es = [jnp.exp2(s - mx) for s in ss]
        den = jnp.sum(es[0], axis=-1, keepdims=True)
        for e in es[1:]:
            den = den + jnp.sum(e, axis=-1, keepdims=True)
        return [e.astype(BF16) for e in es], den

    def attend(bounded):
        y = jnp.zeros((tq, GROUP_W), F32)
        n_units = 2 * N_HEADS_B
        pending = {u: scores(u) for u in range(min(ATTN_LOOKAHEAD, n_units))}
        parts = {}
        for u in range(n_units):
            if u + ATTN_LOOKAHEAD < n_units:
                pending[u + ATTN_LOOKAHEAD] = scores(u + ATTN_LOOKAHEAD)
            parts[u] = softmax_parts(pending.pop(u), bounded)
            if u % 2 == 1:
                h = u // 2
                (num1, den1), (num2, den2) = parts.pop(u - 1), parts.pop(u)
                coef = (lam * den1 / den2).astype(BF16)
                o = jnp.zeros((tq, GROUP_W), F32)
                for e1, e2, v in zip(num1, num2, vs):
                    o = o + jnp.dot(e1 - coef * e2, v, preferred_element_type=F32)
                o = o * (1.0 / den1)
                y = jnp.where((lane >= h * HEAD_V_B) & (lane < (h + 1) * HEAD_V_B), o, y)
        ss = jnp.dot((y * y).astype(BF16), gsum_ref[...], preferred_element_type=F32)
        o_ref[0] = (y * lax.rsqrt(ss * (1.0 / HEAD_V_B) + EPS) * gpost_ref[...]).astype(BF16)

    small = score_bound <= MAX_FIXED_SHIFT

    @pl.when(small)
    def _():
        attend(True)

    @pl.when(jnp.logical_not(small))
    def _():
        attend(False)


def _attention(lam, p_q, kv_sources, gpost, gsum64, tq):
    bn, sq, _ = p_q.shape
    g = GROUP_W
    kv_specs, kv_args = [], []
    for p in kv_sources:
        sk = p.shape[1]
        kv_specs += [pl.BlockSpec((1, sk, g), lambda b, i: (b, 0, 3)),
                     pl.BlockSpec((1, sk, g), lambda b, i: (b, 0, 4))]
        kv_args += [p, p]
    return pl.pallas_call(
        functools.partial(_attn_kernel, n_seg=len(kv_sources)),
        grid=(bn, sq // tq),
        in_specs=[pl.BlockSpec(memory_space=pltpu.SMEM),
                  pl.BlockSpec((1, tq, g), lambda b, i: (b, i, 2))] + kv_specs
                 + [pl.BlockSpec((1, g), lambda b, i: (0, 0)),
                    pl.BlockSpec((g, g), lambda b, i: (0, 0))],
        out_specs=pl.BlockSpec((1, tq, g), lambda b, i: (b, i, 0)),
        out_shape=jax.ShapeDtypeStruct((bn, sq, g), BF16),
        compiler_params=_cparams("parallel", "arbitrary"),
        name="diff_attention",
    )(lam, p_q, *kv_args, gpost, gsum64)


def _out_kernel(ya_ref, yb_ref, yc_ref, yd_ref, x_ref, g1_ref, sc2_ref, sh2_ref, n2g_ref,
                wo_ref, wrh_ref, wrl_ref, br_ref, xo_ref, h2_ref, gate_ref):
    g = GROUP_W
    acc = jnp.dot(ya_ref[0], wo_ref[0:g, :], preferred_element_type=F32)
    acc = acc + jnp.dot(yb_ref[0], wo_ref[g:2 * g, :], preferred_element_type=F32)
    acc = acc + jnp.dot(yc_ref[0], wo_ref[2 * g:3 * g, :], preferred_element_type=F32)
    acc = acc + jnp.dot(yd_ref[0], wo_ref[3 * g:4 * g, :], preferred_element_type=F32)
    xn = x_ref[0] + g1_ref[0] * acc
    xo_ref[0] = xn
    ms = jnp.mean(xn * xn, axis=-1, keepdims=True)
    h2 = (xn * lax.rsqrt(ms + EPS) * n2g_ref[...]) * (1.0 + sc2_ref[0]) + sh2_ref[0]
    hh = h2.astype(BF16)
    h2_ref[0] = hh
    hl = (h2 - hh.astype(F32)).astype(BF16)
    logits = (jnp.dot(hh, wrh_ref[...], preferred_element_type=F32)
              + jnp.dot(hl, wrh_ref[...], preferred_element_type=F32)
              + jnp.dot(hh, wrl_ref[...], preferred_element_type=F32)) + br_ref[...]
    tm = logits.shape[0]
    lane = lax.broadcasted_iota(jnp.int32, (tm, GATE_LANES), 1)
    neg_inf = jnp.float32(-jnp.inf)
    l = jnp.where(lane < N_EXPERTS, logits, neg_inf)
    vals, sels = [], []
    for _ in range(TOP_K):
        m = jnp.max(l, axis=-1, keepdims=True)
        idx = jnp.min(jnp.where(l == m, lane, GATE_LANES), axis=-1, keepdims=True)
        sel = lane == idx
        vals.append(m)
        sels.append(sel)
        l = jnp.where(sel, neg_inf, l)
    es = [jnp.exp(vk - vals[0]) for vk in vals]
    inv = 1.0 / (es[0] + es[1] + es[2] + es[3])
    gates = jnp.zeros((tm, GATE_LANES), F32)
    for sel, ek in zip(sels, es):
        gates = jnp.where(sel, ek * inv, gates)
    gate_ref[0] = gates


def _out_proj(ya, yb, yc, yd, x, g1, sc2, sh2, n2g, wo_bf, wr_hi, wr_lo, br, tm):
    bn, sn, d = x.shape
    g = GROUP_W
    yspec = pl.BlockSpec((1, tm, g), lambda b, s: (b, s, 0))
    vec_spec = pl.BlockSpec((1, 1, d), lambda b, s: (b, 0, 0))
    xspec = pl.BlockSpec((1, tm, d), lambda b, s: (b, s, 0))
    return pl.pallas_call(
        _out_kernel,
        grid=(bn, sn // tm),
        in_specs=[yspec, yspec, yspec, yspec, xspec, vec_spec, vec_spec, vec_spec,
                  pl.BlockSpec((1, d), lambda b, s: (0, 0)),
                  pl.BlockSpec((d, d), lambda b, s: (0, 0)),
                  pl.BlockSpec((d, GATE_LANES), lambda b, s: (0, 0)),
                  pl.BlockSpec((d, GATE_LANES), lambda b, s: (0, 0)),
                  pl.BlockSpec((1, GATE_LANES), lambda b, s: (0, 0))],
        out_specs=[xspec, xspec, pl.BlockSpec((1, tm, GATE_LANES), lambda b, s: (b, s, 0))],
        out_shape=[jax.ShapeDtypeStruct((bn, sn, d), F32),
                   jax.ShapeDtypeStruct((bn, sn, d), BF16),
                   jax.ShapeDtypeStruct((bn, sn, GATE_LANES), F32)],
        compiler_params=_cparams("parallel", "arbitrary"),
        name="out_proj",
    )(ya, yb, yc, yd, x, g1, sc2, sh2, n2g, wo_bf, wr_hi, wr_lo, br)


def _mask_bf16(m):
    return jnp.where(m, 1.0, 0.0).astype(BF16)


def _dispatch_kernel(tab_ref, h2_ref, g_ref, *rest):
    xs_hbm, buf, sem, cnt_s = rest[-4:]
    b = pl.program_id(0)
    nb = pl.num_programs(0)
    slot = b % 2
    ch = CHUNK_ROWS
    tb = TOKEN_BLOCK
    gl = GATE_LANES

    def chunk_copy(s, src, dst):
        return pltpu.make_async_copy(buf.at[s, pl.ds(src, ch), :], xs_hbm.at[pl.ds(dst, ch), :], sem.at[s])

    def wait_chunks(n, s):
        def body(c, carry):
            chunk_copy(s, 0, 0).wait()
            return carry
        lax.fori_loop(0, n, body, 0)

    @pl.when(b >= 2)
    def _():
        wait_chunks(cnt_s[slot], slot)

    mt = g_ref[...].T > 0.0
    mt_bf = _mask_bf16(mt)
    ti = lax.broadcasted_iota(jnp.int32, (tb, tb), 0)
    tj = lax.broadcasted_iota(jnp.int32, (tb, tb), 1)
    rank = jnp.dot(mt_bf, _mask_bf16(ti < tj), preferred_element_type=F32)
    cnt = jnp.sum(jnp.where(mt, 1.0, 0.0), axis=1, keepdims=True)
    nch = jnp.floor((cnt + (ch - 1.0)) * (1.0 / ch))
    ei = lax.broadcasted_iota(jnp.int32, (gl, gl), 0)
    ej = lax.broadcasted_iota(jnp.int32, (gl, gl), 1)
    loc = ch * jnp.dot(_mask_bf16(ej < ei), jnp.broadcast_to(nch, (gl, gl)).astype(BF16),
                       preferred_element_type=F32)[:, 0:1]
    order = jnp.dot(_mask_bf16(ej <= ei), mt_bf, preferred_element_type=F32)
    dest = loc + rank
    dks = []
    for k in range(TOP_K):
        selk = mt & (order == float(k + 1))
        dk = jnp.sum(jnp.where(selk, dest + 1.0, 0.0), axis=0, keepdims=True) - 1.0
        dks.append(dk.astype(jnp.int32))
    h2 = h2_ref[...]
    for rg in range(LOCAL_ROWS // ROW_GROUP):
        ri = lax.broadcasted_iota(jnp.int32, (ROW_GROUP, tb), 0) + rg * ROW_GROUP
        hit = (ri == dks[0]) | (ri == dks[1]) | (ri == dks[2]) | (ri == dks[3])
        srt = jnp.dot(_mask_bf16(hit), h2, preferred_element_type=F32)
        buf[slot, rg * ROW_GROUP:(rg + 1) * ROW_GROUP, :] = srt.astype(BF16)

    n = tab_ref[0, 0, TABLE_W - 1]
    cnt_s[slot] = n

    def issue(c, carry):
        dst = pl.multiple_of(tab_ref[0, 0, c] * ch, ch)
        chunk_copy(slot, pl.multiple_of(c * ch, ch), dst).start()
        return carry
    lax.fori_loop(0, n, issue, 0)

    @pl.when(b == nb - 1)
    def _():
        wait_chunks(n, slot)

        @pl.when(b >= 1)
        def _():
            wait_chunks(cnt_s[1 - slot], 1 - slot)


def _dispatch(table, blk0, h2, gates, rtot, xs_prev=None):
    n, d = h2.shape
    nb = n // TOKEN_BLOCK
    in_specs = [pl.BlockSpec((1, 1, TABLE_W), lambda b: (b + blk0, 0, 0), memory_space=pltpu.SMEM),
                pl.BlockSpec((TOKEN_BLOCK, d), lambda b: (b, 0)),
                pl.BlockSpec((TOKEN_BLOCK, GATE_LANES), lambda b: (b, 0))]
    args = [table, h2, gates]
    aliases = {}
    if xs_prev is not None:
        in_specs.append(pl.BlockSpec(memory_space=pl.ANY))
        args.append(xs_prev)
        aliases = {3: 0}
    return pl.pallas_call(
        _dispatch_kernel,
        grid=(nb,),
        in_specs=in_specs,
        out_specs=pl.BlockSpec(memory_space=pl.ANY),
        out_shape=jax.ShapeDtypeStruct((rtot, d), BF16),
        scratch_shapes=[pltpu.VMEM((2, LOCAL_ROWS, d), BF16),
                        pltpu.SemaphoreType.DMA((2,)),
                        pltpu.SMEM((2,), jnp.int32)],
        input_output_aliases=aliases,
        compiler_params=_cparams("arbitrary"),
        name="moe_dispatch",
    )(*args)


def _combine_kernel(tab_ref, tabn_ref, x_ref, g2_ref, g_ref, y_hbm, o_ref, ybuf, sem, cnt_s):
    ch = CHUNK_ROWS
    tb = TOKEN_BLOCK
    gl = GATE_LANES
    b = pl.program_id(0)
    slot = b % 2
    n = tab_ref[0, 0, TABLE_W - 1]

    def chunk_copy(s, src, dst):
        return pltpu.make_async_copy(y_hbm.at[pl.ds(src, ch), :], ybuf.at[s, pl.ds(dst, ch), :], sem.at[s])

    def fetch_block(tref, s):
        def issue(c, carry):
            src = pl.multiple_of(tref[0, 0, c] * ch, ch)
            chunk_copy(s, src, pl.multiple_of(c * ch, ch)).start()
            return carry
        cnt_s[s] = tref[0, 0, TABLE_W - 1]
        lax.fori_loop(0, cnt_s[s], issue, 0)

    @pl.when(b == 0)
    def _():
        fetch_block(tab_ref, slot)

    fetch_block(tabn_ref, 1 - slot)

    g = g_ref[...]
    m = g > 0.0
    m_bf = _mask_bf16(m)
    ti = lax.broadcasted_iota(jnp.int32, (tb, tb), 0)
    tj = lax.broadcasted_iota(jnp.int32, (tb, tb), 1)
    rank = jnp.dot(_mask_bf16(tj < ti), m_bf, preferred_element_type=F32)
    cnt = jnp.sum(jnp.where(m, 1.0, 0.0), axis=0, keepdims=True)
    nch = jnp.floor((cnt + (ch - 1.0)) * (1.0 / ch))
    ei = lax.broadcasted_iota(jnp.int32, (gl, gl), 0)
    ej = lax.broadcasted_iota(jnp.int32, (gl, gl), 1)
    loc = ch * jnp.dot(jnp.broadcast_to(nch, (8, gl)).astype(BF16), _mask_bf16(ei < ej),
                       preferred_element_type=F32)[0:1, :]
    order = jnp.dot(m_bf, _mask_bf16(ei <= ej), preferred_element_type=F32)
    dest = loc + rank
    dks, gks = [], []
    for k in range(TOP_K):
        selk = m & (order == float(k + 1))
        dk = jnp.sum(jnp.where(selk, dest + 1.0, 0.0), axis=1, keepdims=True) - 1.0
        dks.append(dk.astype(jnp.int32))
        gks.append(jnp.sum(jnp.where(selk, g, 0.0), axis=1, keepdims=True))

    def wait_one(c, carry):
        chunk_copy(slot, 0, 0).wait()
        return carry
    lax.fori_loop(0, cnt_s[slot], wait_one, 0)

    acc = jnp.zeros(x_ref.shape, F32)
    for rg in range(LOCAL_ROWS // ROW_GROUP):
        li = lax.broadcasted_iota(jnp.int32, (tb, ROW_GROUP), 1) + rg * ROW_GROUP
        cmat = jnp.zeros((tb, ROW_GROUP), F32)
        for dk, gk in zip(dks, gks):
            cmat = cmat + jnp.where(li == dk, gk, 0.0)
        ri = lax.broadcasted_iota(jnp.int32, (ROW_GROUP, 1), 0) + rg * ROW_GROUP
        yv = ybuf[slot, rg * ROW_GROUP:(rg + 1) * ROW_GROUP, :]
        yv = jnp.where(ri < n * ch, yv, jnp.zeros_like(yv))
        acc = acc + jnp.dot(cmat.astype(BF16), yv, preferred_element_type=F32)
    o_ref[...] = x_ref[...] + g2_ref[0] * acc


def _combine(table, blk0, x, g2, gates, ys):
    bn, sn, d = x.shape
    n = bn * sn
    per_seq = sn // TOKEN_BLOCK
    n_blocks = n // TOKEN_BLOCK
    empty_row = table.shape[0] - 1
    out = pl.pallas_call(
        _combine_kernel,
        grid=(n_blocks,),
        in_specs=[pl.BlockSpec((1, 1, TABLE_W), lambda b: (b + blk0, 0, 0), memory_space=pltpu.SMEM),
                  pl.BlockSpec((1, 1, TABLE_W),
                               lambda b: (jnp.where(b + 1 < n_blocks, b + blk0 + 1, empty_row), 0, 0),
                               memory_space=pltpu.SMEM),
                  pl.BlockSpec((TOKEN_BLOCK, d), lambda b: (b, 0)),
                  pl.BlockSpec((1, 1, d), lambda b: (b // per_seq, 0, 0)),
                  pl.BlockSpec((TOKEN_BLOCK, GATE_LANES), lambda b: (b, 0)),
                  pl.BlockSpec(memory_space=pl.ANY)],
        out_specs=pl.BlockSpec((TOKEN_BLOCK, d), lambda b: (b, 0)),
        out_shape=jax.ShapeDtypeStruct((n, d), F32),
        scratch_shapes=[pltpu.VMEM((2, LOCAL_ROWS, d), BF16),
                        pltpu.SemaphoreType.DMA((2,)),
                        pltpu.SMEM((2,), jnp.int32)],
        compiler_params=_cparams("arbitrary"),
        name="moe_combine",
    )(table, table, x.reshape(n, d), g2, gates.reshape(n, GATE_LANES), ys)
    return out.reshape(bn, sn, d)


TILE_ACTIVE, TILE_FIRST, TILE_SLOT, TILE_COLD = 1, 2, 4, 8


def _expert_kernel(te_ref, tf_ref, nv_ref, nx_ref, x_ref, wgu_hbm, bgu_ref, wdn_hbm, bdn_ref, y_ref,
                   wgu_st, wdn_st, wgu_s, wdn_s, sem, *, layer):
    i = pl.program_id(0)
    flag = tf_ref[i]
    d, f2 = wgu_s.shape
    f = f2 // 2
    rows = 128

    def weight_copies(e, s):
        return (pltpu.make_async_copy(wgu_hbm.at[layer, e], wgu_st.at[s], sem.at[0, s]),
                pltpu.make_async_copy(wdn_hbm.at[layer, e], wdn_st.at[s], sem.at[1, s]))

    @pl.when((flag & TILE_FIRST) != 0)
    def _():
        slot = (flag // TILE_SLOT) & 1

        @pl.when((flag & TILE_COLD) != 0)
        def _():
            for cp in weight_copies(te_ref[i], slot):
                cp.start()

        for cp in weight_copies(te_ref[i], slot):
            cp.wait()
        nxt = nx_ref[i]

        @pl.when(nxt >= 0)
        def _():
            for cp in weight_copies(nxt, 1 - slot):
                cp.start()

        def cast_gu(r, c):
            r0 = pl.multiple_of(r * rows, rows)
            wgu_s[pl.ds(r0, rows), :] = wgu_st[slot, pl.ds(r0, rows), :].astype(BF16)
            return c
        lax.fori_loop(0, d // rows, cast_gu, 0)

        def cast_dn(r, c):
            r0 = pl.multiple_of(r * rows, rows)
            wdn_s[pl.ds(r0, rows), :] = wdn_st[slot, pl.ds(r0, rows), :].astype(BF16)
            return c
        lax.fori_loop(0, f // rows, cast_dn, 0)

    @pl.when((flag & TILE_ACTIVE) != 0)
    def _():
        ri = lax.broadcasted_iota(jnp.int32, (x_ref.shape[0], 1), 0)
        x = x_ref[...]
        x = jnp.where(ri < nv_ref[i], x, jnp.zeros_like(x))
        gu = jnp.dot(x, wgu_s[...], preferred_element_type=F32) + bgu_ref[0]
        gl = jnp.minimum(gu[:, :f], SWIGLU_LIMIT)
        up = jnp.clip(gu[:, f:], -SWIGLU_LIMIT, SWIGLU_LIMIT)
        hid = (up + 1.0) * gl * jax.nn.sigmoid(SWIGLU_ALPHA * gl)
        y = jnp.dot(hid.astype(BF16), wdn_s[...], preferred_element_type=F32) + bdn_ref[0]
        y_ref[...] = y.astype(BF16)

    @pl.when((flag & TILE_ACTIVE) == 0)
    def _():
        y_ref[...] = jnp.zeros_like(y_ref)


def _experts(tile_expert, tile_flag, tile_rows, tile_next, xs, w_gu, b_gu, w_dn, b_dn, layer, tm):
    rtot, d = xs.shape
    _, ne, _, f2 = w_gu.shape
    f = f2 // 2
    grid_spec = pltpu.PrefetchScalarGridSpec(
        num_scalar_prefetch=4,
        grid=(rtot // tm,),
        in_specs=[pl.BlockSpec((tm, d), lambda i, te, tf, nv, nx: (i, 0)),
                  pl.BlockSpec(memory_space=pl.ANY),
                  pl.BlockSpec((1, 1, f2), lambda i, te, tf, nv, nx: (te[i], 0, 0)),
                  pl.BlockSpec(memory_space=pl.ANY),
                  pl.BlockSpec((1, 1, d), lambda i, te, tf, nv, nx: (te[i], 0, 0))],
        out_specs=pl.BlockSpec((tm, d), lambda i, te, tf, nv, nx: (i, 0)),
        scratch_shapes=[pltpu.VMEM((2, d, f2), F32), pltpu.VMEM((2, f, d), F32),
                        pltpu.VMEM((d, f2), BF16), pltpu.VMEM((f, d), BF16),
                        pltpu.SemaphoreType.DMA((2, 2))],
    )
    return pl.pallas_call(
        functools.partial(_expert_kernel, layer=layer),
        grid_spec=grid_spec,
        out_shape=jax.ShapeDtypeStruct((rtot, d), BF16),
        compiler_params=_cparams("arbitrary"),
        name="experts",
    )(tile_expert, tile_flag, tile_rows, tile_next, xs, w_gu, b_gu.reshape(ne, 1, f2), w_dn,
      b_dn.reshape(ne, 1, d))


def _moe_layout(gate_streams, tm):
    n = sum(gs.shape[0] for gs in gate_streams)
    nb = n // TOKEN_BLOCK
    ne = N_EXPERTS
    ch = CHUNK_ROWS
    tmc = tm // ch
    n_tiles = -(-(n * TOP_K + nb * ne * (ch - 1) + ne * (tm - ch)) // tm)
    i32 = jnp.int32
    cnt = jnp.concatenate(
        [jnp.sum((gs[:, :ne] > 0.0).astype(i32).reshape(-1, TOKEN_BLOCK, ne), axis=1) for gs in gate_streams], axis=0)
    nch = (cnt + ch - 1) // ch
    loc_end = jnp.cumsum(nch, axis=1)
    loc_off = loc_end - nch
    exp_ch = jnp.sum(nch, axis=0)
    exp_pad = ((exp_ch + tmc - 1) // tmc) * tmc
    exp_end = jnp.cumsum(exp_pad)
    exp_off = exp_end - exp_pad
    seg_off = exp_off[None, :] + jnp.cumsum(nch, axis=0) - nch
    c = jnp.arange(MAX_CHUNKS, dtype=i32)
    e_of_c = jnp.minimum(jnp.sum((loc_end[:, None, :] <= c[None, :, None]).astype(i32), axis=2), ne - 1)
    onehot = (e_of_c[:, :, None] == jnp.arange(ne, dtype=i32)).astype(i32)
    gdst = jnp.sum(onehot * (seg_off - loc_off)[:, None, :], axis=2) + c[None, :]
    nblk = loc_end[:, ne - 1]
    gdst = jnp.where(c[None, :] < nblk[:, None], gdst, 0)
    table = jnp.concatenate([gdst, jnp.zeros((nb, TABLE_W - MAX_CHUNKS - 1), i32), nblk[:, None]], axis=1)
    table = jnp.concatenate([table, jnp.zeros((1, TABLE_W), i32)], axis=0)
    tile_start = jnp.arange(n_tiles, dtype=i32) * tmc
    te = jnp.minimum(jnp.sum((exp_end[None, :] <= tile_start[:, None]).astype(i32), axis=1), ne - 1)
    te_hot = (te[:, None] == jnp.arange(ne, dtype=i32)).astype(i32)
    t_off = jnp.sum(te_hot * exp_off[None, :], axis=1)
    t_len = jnp.sum(te_hot * exp_ch[None, :], axis=1)
    total = exp_end[ne - 1]
    active = tile_start < total
    first = active & (tile_start == t_off)
    tile_rows = jnp.clip((t_len - (tile_start - t_off)) * ch, 0, tm)
    last_active = (tile_start + tmc == total).astype(i32)
    te = jnp.where(active, te, jnp.sum(last_active * te))
    group = jnp.cumsum(first.astype(i32)) - 1
    ei = jnp.arange(ne, dtype=i32)
    later = (ei[None, :] > ei[:, None]) & (exp_ch[None, :] > 0)
    next_e = jnp.min(jnp.where(later, ei[None, :], ne), axis=1)
    next_e = jnp.where(next_e < ne, next_e, -1)
    tile_next = jnp.sum(te_hot * next_e[None, :], axis=1)
    tile_flag = (TILE_ACTIVE * active.astype(i32) + TILE_FIRST * first.astype(i32) + TILE_SLOT * (group & 1)
                 + TILE_COLD * (first & (group == 0)).astype(i32))
    return table.reshape(nb + 1, 1, TABLE_W), (te, tile_flag, tile_rows, tile_next), n_tiles * tm


def _rope_tables(sn, scale_q, roped):
    if not roped:
        ones = jnp.ones((sn, GROUP_W), F32)
        zeros = jnp.zeros((sn, GROUP_W), F32)
        return ones * scale_q, zeros, ones, zeros
    pos = jnp.arange(sn)
    row = (pos // GRID_W).astype(F32)
    col = (pos % GRID_W).astype(F32)
    nf = HEAD_QK_B // 4
    inv = ROPE_BASE ** (-jnp.arange(nf, dtype=F32) / nf)
    ar = row[:, None] * inv
    ac = col[:, None] * inv
    cos32 = jnp.concatenate([jnp.cos(ar), jnp.cos(ar), jnp.cos(ac), jnp.cos(ac)], axis=1)
    sin32 = jnp.concatenate([-jnp.sin(ar), jnp.sin(ar), -jnp.sin(ac), jnp.sin(ac)], axis=1)
    reps = GROUP_W // HEAD_QK_B
    cos = jnp.tile(cos32, (1, reps))
    sin = jnp.tile(sin32, (1, reps))
    return cos * scale_q, sin * scale_q, cos, sin


def kernel(x, c, ctx, c_ctx, w_mod, b_mod, norm1_g, norm2_g, w_in, a_vnorm_g, a_ws, a_bs, b_qnorm_g, b_knorm_g, b_lam_q1, b_lam_k1, b_lam_q2, b_lam_k2, b_subln_g, c_conv_w, c_conv_b, c_ln_g, c_ln_b, d_conv_w, w_out, router_w, router_b, exp_w_gu, exp_b_gu, exp_w_dn, exp_b_dn):
    bsz, seq, dm = x.shape
    ctx_len = ctx.shape[1]
    depth = w_in.shape[0]
    g = GROUP_W
    tm_lat = min(512, seq)
    tm_ctx = min(512, ctx_len)
    tq_lat = min(512, seq)
    tq_ctx = min(256, ctx_len)
    tm_moe = 512
    assert seq % TOKEN_BLOCK == 0 and ctx_len % TOKEN_BLOCK == 0

    n_mod_rows = -(-(bsz + 1) // 8) * 8
    cc = jnp.concatenate([c, c_ctx[None, :], jnp.zeros((n_mod_rows - bsz - 1, dm), F32)], axis=0)
    mods = _modulation(cc, w_mod, b_mod)

    gsum32 = _group_ones(g, HEAD_QK_B)
    gsum64 = _group_ones(g, HEAD_V_B)
    scale_q = HEAD_QK_B ** -0.5 * math.log2(math.e)
    tabs_lat = _rope_tables(seq, scale_q, True)
    tabs_ctx = _rope_tables(ctx_len, scale_q, False)

    xc = ctx
    for l in range(depth):
        last = l == depth - 1
        lam_init = 0.8 - 0.6 * math.exp(-0.3 * l)
        lam = (jnp.exp(jnp.sum(b_lam_q1[l] * b_lam_k1[l])) - jnp.exp(jnp.sum(b_lam_q2[l] * b_lam_k2[l]))
               + lam_init).astype(F32)
        score_bound = 1.01 * scale_q * HEAD_QK_B * jnp.max(jnp.abs(b_qnorm_g[l])) * jnp.max(jnp.abs(b_knorm_g[l]))
        lam = jnp.stack([lam, score_bound.astype(F32)])
        m_lat = mods[l, :bsz].reshape(bsz, 6, 1, dm)
        m_ctx = jnp.broadcast_to(mods[l, bsz].reshape(1, 6, 1, dm), (bsz, 6, 1, dm))
        sh1, sc1, g1, sh2, sc2, g2 = [m_lat[:, i] for i in range(6)]
        sh1c, sc1c, g1c, sh2c, sc2c, g2c = [m_ctx[:, i] for i in range(6)]

        w_in_bf = w_in[l].astype(BF16)
        n1g = norm1_g[l][None, :]
        n2g = norm2_g[l][None, :]
        reps = g // HEAD_QK_B // 2
        qg = jnp.tile(b_qnorm_g[l].reshape(1, 2 * HEAD_QK_B), (1, reps))
        kg = jnp.tile(b_knorm_g[l].reshape(1, 2 * HEAD_QK_B), (1, reps))
        gpost = jnp.tile(b_subln_g[l][None, :], (1, N_HEADS_B)) * (1.0 - lam_init)
        vg = a_vnorm_g[l][None, :]
        ws_bf = a_ws[l].astype(BF16)
        bs_exp = jnp.repeat(a_bs[l].T, HEAD_A, axis=1)
        mix_params = (vg, ws_bf, bs_exp, gsum64, c_conv_w[l], c_conv_b[l][None, :], c_ln_g[l][None, :],
                      c_ln_b[l][None, :], d_conv_w[l])
        wo_bf = w_out[l].astype(BF16)
        wr = jnp.pad(router_w[l], ((0, 0), (0, GATE_LANES - N_EXPERTS)))
        wr_hi = wr.astype(BF16)
        wr_lo = (wr - wr_hi.astype(F32)).astype(BF16)
        br = jnp.pad(router_b[l], (0, GATE_LANES - N_EXPERTS))[None, :]

        p_lat = _in_proj(x, sc1, sh1, n1g, w_in_bf, gsum32, qg, kg, tabs_lat, tm_lat)
        p_ctx = _in_proj(xc, sc1c, sh1c, n1g, w_in_bf, gsum32, qg, kg, tabs_ctx, tm_ctx)

        yb = _attention(lam, p_lat, (p_lat, p_ctx), gpost, gsum64, tq_lat)
        ya, yc, yd = _mixers(p_lat, *mix_params)
        x, h2, gates = _out_proj(ya, yb, yc, yd, x, g1, sc2, sh2, n2g, wo_bf, wr_hi, wr_lo, br, tm_lat)
        moe_w = (exp_w_gu, exp_b_gu[l], exp_w_dn, exp_b_dn[l], l, tm_moe)
        h2f = h2.reshape(-1, dm)
        gatesf = gates.reshape(-1, GATE_LANES)
        if last:
            table, tiles, rtot = _moe_layout((gatesf,), tm_moe)
            xs = _dispatch(table, 0, h2f, gatesf, rtot)
            ys = _experts(*tiles, xs, *moe_w)
            x = _combine(table, 0, x, g2, gates, ys)
        else:
            ybc = _attention(lam, p_ctx, (p_ctx,), gpost, gsum64, tq_ctx)
            yac, ycc, ydc = _mixers(p_ctx, *mix_params)
            xc, h2c, gatesc = _out_proj(yac, ybc, ycc, ydc, xc, g1c, sc2c, sh2c, n2g, wo_bf, wr_hi, wr_lo, br,
                                        tm_ctx)
            gatescf = gatesc.reshape(-1, GATE_LANES)
            blk_ctx = bsz * seq // TOKEN_BLOCK
            table, tiles, rtot = _moe_layout((gatesf, gatescf), tm_moe)
            xs = _dispatch(table, 0, h2f, gatesf, rtot)
            xs = _dispatch(table, blk_ctx, h2c.reshape(-1, dm), gatescf, rtot, xs_prev=xs)
            ys = _experts(*tiles, xs, *moe_w)
            x = _combine(table, 0, x, g2, gates, ys)
            xc = _combine(table, blk_ctx, xc, g2c, gatesc, ys)
    return x
```

```python
import functools
import math

import jax
import jax.numpy as jnp
from jax import lax
from jax.experimental import pallas as pl
from jax.experimental.pallas import tpu as pltpu

F32 = jnp.float32
BF16 = jnp.bfloat16

GROUP_W = 256
N_SPLITS = 10
N_HEADS_A = 4
HEAD_A = 64
CHUNK = 128
N_HEADS_B = 4
HEAD_V_B = 64
HEAD_QK_B = 32
GRID_W = 64
ROPE_BASE = 10000.0
CONV_C = 31
CONV_D = 3
N_EXPERTS = 32
TOP_K = 4
SWIGLU_LIMIT = 7.0
SWIGLU_ALPHA = 1.702
EPS = 1e-6

GATE_LANES = 128
CONV_HALO = 16
CONV_SPAN = CHUNK + 8 * ((CONV_HALO - CONV_C // 2 + CONV_C - 1) // 8)
VMEM_LIMIT = 56 * 1024 * 1024
MAX_FIXED_SHIFT = 60.0
ATTN_LOOKAHEAD = 2

TOKEN_BLOCK = 256
CHUNK_ROWS = 16
ROW_GROUP = 256
LOCAL_ROWS = -(-(TOKEN_BLOCK * TOP_K + N_EXPERTS * (CHUNK_ROWS - 1)) // ROW_GROUP) * ROW_GROUP
MAX_CHUNKS = LOCAL_ROWS // CHUNK_ROWS
COPY_CHUNKS = (4, 2, 1)
COPY_SLOTS = (MAX_CHUNKS // 4, N_EXPERTS, N_EXPERTS)
COPY_BASE = (0, COPY_SLOTS[0], COPY_SLOTS[0] + COPY_SLOTS[1])
TABLE_W = 128
COUNT_BASE = TABLE_W - len(COPY_CHUNKS)
assert MAX_CHUNKS <= 256 and sum(COPY_SLOTS) <= COUNT_BASE


def _cparams(*sem):
    return pltpu.CompilerParams(dimension_semantics=sem, vmem_limit_bytes=VMEM_LIMIT)


def _group_ones(width, group):
    i = jnp.arange(width) // group
    return (i[:, None] == i[None, :]).astype(BF16)


def _mod_kernel(c_ref, w_ref, b_ref, o_ref):
    c = c_ref[...]
    a = (c * jax.nn.sigmoid(c)).astype(BF16)
    o_ref[0] = jnp.dot(a, w_ref[0].astype(BF16), preferred_element_type=F32) + b_ref[0]


def _modulation(cc, w_mod, b_mod):
    nl, d, d6 = w_mod.shape
    r = cc.shape[0]
    return pl.pallas_call(
        _mod_kernel,
        grid=(nl, d6 // d),
        in_specs=[pl.BlockSpec((r, d), lambda l, j: (0, 0)),
                  pl.BlockSpec((1, d, d), lambda l, j: (l, 0, j)),
                  pl.BlockSpec((1, 1, d), lambda l, j: (l, 0, j))],
        out_specs=pl.BlockSpec((1, r, d), lambda l, j: (l, 0, j)),
        out_shape=jax.ShapeDtypeStruct((nl, r, d6), F32),
        compiler_params=_cparams("arbitrary", "arbitrary"),
        name="modulation",
    )(cc, w_mod, b_mod.reshape(nl, 1, d6))


def _rope_norm(t, gsum, gain, cos, sin, lo_lane):
    ss = jnp.dot((t * t).astype(BF16), gsum, preferred_element_type=F32)
    tn = t * lax.rsqrt(ss * (1.0 / HEAD_QK_B) + EPS) * gain
    rot = jnp.where(lo_lane, pltpu.roll(tn, GROUP_W - 8, 1), pltpu.roll(tn, 8, 1))
    return tn * cos + rot * sin


def _in_kernel(x_ref, sc_ref, sh_ref, g_ref, w_ref, gsum_ref, qg_ref, kg_ref,
               cq_ref, sq_ref, ck_ref, sk_ref, o_ref):
    x = x_ref[0]
    ms = jnp.mean(x * x, axis=-1, keepdims=True)
    h = (x * lax.rsqrt(ms + EPS) * g_ref[...]) * (1.0 + sc_ref[0]) + sh_ref[0]
    p = jnp.dot(h.astype(BF16), w_ref[...], preferred_element_type=F32)
    g = GROUP_W
    lane = lax.broadcasted_iota(jnp.int32, (1, g), 1)
    lo_lane = (lane & 8) == 0
    gsum = gsum_ref[...]
    q = _rope_norm(p[:, 2 * g:3 * g], gsum, qg_ref[...], cq_ref[...], sq_ref[...], lo_lane)
    k = _rope_norm(p[:, 3 * g:4 * g], gsum, kg_ref[...], ck_ref[...], sk_ref[...], lo_lane)
    o_ref[0, :, 0:2 * g] = p[:, 0:2 * g].astype(BF16)
    o_ref[0, :, 2 * g:3 * g] = q.astype(BF16)
    o_ref[0, :, 3 * g:4 * g] = k.astype(BF16)
    o_ref[0, :, 4 * g:] = p[:, 4 * g:].astype(BF16)


def _in_proj(x, sc, sh, g, w_bf, gsum32, qg, kg, tabs, tm):
    bn, sn, d = x.shape
    d_in = w_bf.shape[1]
    tab_spec = pl.BlockSpec((tm, GROUP_W), lambda b, s: (s, 0))
    vec_spec = pl.BlockSpec((1, 1, d), lambda b, s: (b, 0, 0))
    row256 = pl.BlockSpec((1, GROUP_W), lambda b, s: (0, 0))
    return pl.pallas_call(
        _in_kernel,
        grid=(bn, sn // tm),
        in_specs=[pl.BlockSpec((1, tm, d), lambda b, s: (b, s, 0)),
                  vec_spec, vec_spec,
                  pl.BlockSpec((1, d), lambda b, s: (0, 0)),
                  pl.BlockSpec((d, d_in), lambda b, s: (0, 0)),
                  pl.BlockSpec((GROUP_W, GROUP_W), lambda b, s: (0, 0)),
                  row256, row256, tab_spec, tab_spec, tab_spec, tab_spec],
        out_specs=pl.BlockSpec((1, tm, d_in), lambda b, s: (b, s, 0)),
        out_shape=jax.ShapeDtypeStruct((bn, sn, d_in), BF16),
        compiler_params=_cparams("parallel", "arbitrary"),
        name="in_proj",
    )(x, sc, sh, g, w_bf, gsum32, qg, kg, *tabs)


def _mix_kernel(au_ref, av_ref, ca_ref, cg_ref, db_ref, dc_ref, dh_ref,
                vg_ref, ws_ref, bs_ref, gsum_ref, cw_ref, cb_ref, lng_ref, lnb_ref, dw_ref,
                ya_ref, yc_ref, yd_ref, zc_ref, zd_ref, zs_ref, *, sn):
    g = GROUP_W
    lane = lax.broadcasted_iota(jnp.int32, (1, g), 1)
    n_chunks = sn // CHUNK

    def gmlp_chunk(ci, carry):
        r0 = pl.multiple_of(ci * CHUNK, CHUNK)
        u = jax.nn.gelu(au_ref[0, pl.ds(r0, CHUNK), :].astype(F32))
        v = jax.nn.gelu(av_ref[0, pl.ds(r0, CHUNK), :].astype(F32))
        ss = jnp.dot((v * v).astype(BF16), gsum_ref[...], preferred_element_type=F32)
        vn = (v * lax.rsqrt(ss * (1.0 / HEAD_A) + EPS) * vg_ref[...]).astype(BF16)
        mixed = bs_ref[...]
        for h in range(N_HEADS_A):
            mh = jnp.dot(ws_ref[h], vn, preferred_element_type=F32)
            hm = (lane >= h * HEAD_A) & (lane < (h + 1) * HEAD_A)
            mixed = mixed + jnp.where(hm, mh, 0.0)
        ya_ref[0, pl.ds(r0, CHUNK), :] = (u * mixed).astype(BF16)
        return carry

    lax.fori_loop(0, n_chunks, gmlp_chunk, 0, unroll=2)

    zeros_halo = jnp.zeros((CONV_HALO, g), F32)
    zc_ref[0:CONV_HALO, :] = zeros_halo
    zc_ref[CONV_HALO + sn:CONV_HALO + sn + CONV_HALO, :] = zeros_halo
    zd_ref[0:CONV_HALO, :] = zeros_halo
    zd_ref[CONV_HALO + sn:CONV_HALO + sn + CONV_HALO, :] = zeros_halo

    def stage_chunk(ci, carry):
        r0 = pl.multiple_of(ci * CHUNK, CHUNK)
        ca = ca_ref[0, pl.ds(r0, CHUNK), :].astype(F32)
        cg = cg_ref[0, pl.ds(r0, CHUNK), :].astype(F32)
        zc_ref[pl.ds(r0 + CONV_HALO, CHUNK), :] = ca * jax.nn.sigmoid(cg)
        dc = dc_ref[0, pl.ds(r0, CHUNK), :].astype(F32)
        dh = dh_ref[0, pl.ds(r0, CHUNK), :].astype(F32)
        zd_ref[pl.ds(r0 + CONV_HALO, CHUNK), :] = dc * dh
        return carry

    lax.fori_loop(0, n_chunks, stage_chunk, 0)

    def conv_chunk(ci, carry):
        r0 = pl.multiple_of(ci * CHUNK, CHUNK)
        win = zc_ref[pl.ds(r0, CHUNK + 2 * CONV_HALO), :]
        acc = jnp.zeros((CHUNK, g), F32) + cb_ref[...]
        base = CONV_HALO - CONV_C // 2
        for ph in range(8):
            taps = [k for k in range(CONV_C) if (base + k) % 8 == ph]
            if taps:
                zs_ref[ph] = win[ph:ph + CONV_SPAN, :]
                for k in taps:
                    a = (base + k) // 8 * 8
                    acc = acc + cw_ref[k:k + 1, :] * zs_ref[ph, a:a + CHUNK, :]
        mu = jnp.mean(acc, axis=-1, keepdims=True)
        dev = acc - mu
        var = jnp.mean(dev * dev, axis=-1, keepdims=True)
        yln = dev * lax.rsqrt(var + EPS) * lng_ref[...] + lnb_ref[...]
        yc_ref[0, pl.ds(r0, CHUNK), :] = (yln * jax.nn.sigmoid(yln)).astype(BF16)

        wind = zd_ref[pl.ds(r0, CHUNK + 2 * CONV_HALO), :]
        accd = jnp.zeros((CHUNK, g), F32)
        based = CONV_HALO - CONV_D // 2
        for k in range(CONV_D):
            accd = accd + dw_ref[k:k + 1, :] * wind[based + k:based + k + CHUNK, :]
        db = db_ref[0, pl.ds(r0, CHUNK), :].astype(F32)
        yd_ref[0, pl.ds(r0, CHUNK), :] = (db * accd).astype(BF16)
        return carry

    lax.fori_loop(0, n_chunks, conv_chunk, 0)


def _mixers(p, vg, ws_bf, bs_exp, gsum64, cw, cb, lng, lnb, dw):
    bn, sn, _ = p.shape
    g = GROUP_W

    def col(j):
        return pl.BlockSpec((1, sn, g), lambda b, j=j: (b, 0, j))

    def full(a):
        nd = a.ndim
        return pl.BlockSpec(a.shape, lambda b, nd=nd: (0,) * nd)

    out_spec = pl.BlockSpec((1, sn, g), lambda b: (b, 0, 0))
    out_sds = jax.ShapeDtypeStruct((bn, sn, g), BF16)
    params = (vg, ws_bf, bs_exp, gsum64, cw, cb, lng, lnb, dw)
    return pl.pallas_call(
        functools.partial(_mix_kernel, sn=sn),
        grid=(bn,),
        in_specs=[col(0), col(1), col(5), col(6), col(7), col(8), col(9)] + [full(a) for a in params],
        out_specs=[out_spec, out_spec, out_spec],
        out_shape=[out_sds, out_sds, out_sds],
        scratch_shapes=[pltpu.VMEM((sn + 2 * CONV_HALO, g), F32),
                        pltpu.VMEM((sn + 2 * CONV_HALO, g), F32),
                        pltpu.VMEM((8, CONV_SPAN, g), F32)],
        compiler_params=_cparams("parallel"),
        name="mixers",
    )(p, p, p, p, p, p, p, *params)


def _attn_kernel(lam_ref, q_ref, *rest, n_seg):
    k_refs = rest[0:2 * n_seg:2]
    v_refs = rest[1:2 * n_seg:2]
    gpost_ref, gsum_ref, o_ref = rest[2 * n_seg:]
    lam = lam_ref[0]
    score_bound = lam_ref[1]
    q = q_ref[0]
    ks = [r[0] for r in k_refs]
    vs = [r[0] for r in v_refs]
    tq = q.shape[0]
    lane = lax.broadcasted_iota(jnp.int32, (1, GROUP_W), 1)

    def scores(u):
        lo = u * HEAD_QK_B
        qm = jnp.where((lane >= lo) & (lane < lo + HEAD_QK_B), q, jnp.zeros_like(q))
        return [lax.dot_general(qm, k, (((1,), (1,)), ((), ())), preferred_element_type=F32) for k in ks]

    def softmax_parts(ss, bounded):
        if bounded:
            mx = score_bound
        else:
            mx = jnp.max(ss[0], axis=-1, keepdims=True)
            for s in ss[1:]:
                mx = jnp.maximum(mx, jnp.max(s, axis=-1, keepdims=True))
        es = [jnp.exp2(s - mx) for s in ss]
        den = jnp.sum(es[0], axis=-1, keepdims=True)
        for e in es[1:]:
            den = den + jnp.sum(e, axis=-1, keepdims=True)
        return [e.astype(BF16) for e in es], den

    def attend(bounded):
        y = jnp.zeros((tq, GROUP_W), F32)
        n_units = 2 * N_HEADS_B
        pending = {u: scores(u) for u in range(min(ATTN_LOOKAHEAD, n_units))}
        parts = {}
        for u in range(n_units):
            if u + ATTN_LOOKAHEAD < n_units:
                pending[u + ATTN_LOOKAHEAD] = scores(u + ATTN_LOOKAHEAD)
            parts[u] = softmax_parts(pending.pop(u), bounded)
            if u % 2 == 1:
                h = u // 2
                (num1, den1), (num2, den2) = parts.pop(u - 1), parts.pop(u)
                coef = (lam * den1 / den2).astype(BF16)
                o = jnp.zeros((tq, GROUP_W), F32)
                for e1, e2, v in zip(num1, num2, vs):
                    o = o + jnp.dot(e1 - coef * e2, v, preferred_element_type=F32)
                o = o * (1.0 / den1)
                y = jnp.where((lane >= h * HEAD_V_B) & (lane < (h + 1) * HEAD_V_B), o, y)
        ss = jnp.dot((y * y).astype(BF16), gsum_ref[...], preferred_element_type=F32)
        o_ref[0] = (y * lax.rsqrt(ss * (1.0 / HEAD_V_B) + EPS) * gpost_ref[...]).astype(BF16)

    small = score_bound <= MAX_FIXED_SHIFT

    @pl.when(small)
    def _():
        attend(True)

    @pl.when(jnp.logical_not(small))
    def _():
        attend(False)


def _attention(lam, p_q, kv_sources, gpost, gsum64, tq):
    bn, sq, _ = p_q.shape
    g = GROUP_W
    kv_specs, kv_args = [], []
    for p in kv_sources:
        sk = p.shape[1]
        kv_specs += [pl.BlockSpec((1, sk, g), lambda b, i: (b, 0, 3)),
                     pl.BlockSpec((1, sk, g), lambda b, i: (b, 0, 4))]
        kv_args += [p, p]
    return pl.pallas_call(
        functools.partial(_attn_kernel, n_seg=len(kv_sources)),
        grid=(bn, sq // tq),
        in_specs=[pl.BlockSpec(memory_space=pltpu.SMEM),
                  pl.BlockSpec((1, tq, g), lambda b, i: (b, i, 2))] + kv_specs
                 + [pl.BlockSpec((1, g), lambda b, i: (0, 0)),
                    pl.BlockSpec((g, g), lambda b, i: (0, 0))],
        out_specs=pl.BlockSpec((1, tq, g), lambda b, i: (b, i, 0)),
        out_shape=jax.ShapeDtypeStruct((bn, sq, g), BF16),
        compiler_params=_cparams("parallel", "arbitrary"),
        name="diff_attention",
    )(lam, p_q, *kv_args, gpost, gsum64)


def _out_kernel(ya_ref, yb_ref, yc_ref, yd_ref, x_ref, g1_ref, sc2_ref, sh2_ref, n2g_ref,
                wo_ref, wrh_ref, wrl_ref, br_ref, xo_ref, h2_ref, gate_ref):
    g = GROUP_W
    tm = x_ref.shape[1]
    n_parts = 2 if tm % 16 == 0 else 1
    rows = tm // n_parts

    def project(r0):
        sl = slice(r0, r0 + rows)
        acc = jnp.dot(ya_ref[0, sl, :], wo_ref[0:g, :], preferred_element_type=F32)
        acc = acc + jnp.dot(yb_ref[0, sl, :], wo_ref[g:2 * g, :], preferred_element_type=F32)
        acc = acc + jnp.dot(yc_ref[0, sl, :], wo_ref[2 * g:3 * g, :], preferred_element_type=F32)
        return acc + jnp.dot(yd_ref[0, sl, :], wo_ref[3 * g:4 * g, :], preferred_element_type=F32)

    def finish(r0, acc):
        sl = slice(r0, r0 + rows)
        xn = x_ref[0, sl, :] + g1_ref[0] * acc
        xo_ref[0, sl, :] = xn
        ms = jnp.mean(xn * xn, axis=-1, keepdims=True)
        h2 = (xn * lax.rsqrt(ms + EPS) * n2g_ref[...]) * (1.0 + sc2_ref[0]) + sh2_ref[0]
        hh = h2.astype(BF16)
        h2_ref[0, sl, :] = hh
        hl = (h2 - hh.astype(F32)).astype(BF16)
        logits = (jnp.dot(hh, wrh_ref[...], preferred_element_type=F32)
                  + jnp.dot(hl, wrh_ref[...], preferred_element_type=F32)
                  + jnp.dot(hh, wrl_ref[...], preferred_element_type=F32)) + br_ref[...]
        lane = lax.broadcasted_iota(jnp.int32, (rows, GATE_LANES), 1)
        neg_inf = jnp.float32(-jnp.inf)
        l = jnp.where(lane < N_EXPERTS, logits, neg_inf)
        vals, sels = [], []
        for _ in range(TOP_K):
            m = jnp.max(l, axis=-1, keepdims=True)
            idx = jnp.min(jnp.where(l == m, lane, GATE_LANES), axis=-1, keepdims=True)
            sel = lane == idx
            vals.append(m)
            sels.append(sel)
            l = jnp.where(sel, neg_inf, l)
        es = [jnp.exp(vk - vals[0]) for vk in vals]
        inv = 1.0 / (es[0] + es[1] + es[2] + es[3])
        gates = jnp.zeros((rows, GATE_LANES), F32)
        for sel, ek in zip(sels, es):
            gates = jnp.where(sel, ek * inv, gates)
        gate_ref[0, sl, :] = gates

    accs = [project(p * rows) for p in range(n_parts)]
    for p, acc in enumerate(accs):
        finish(p * rows, acc)


def _out_proj(ya, yb, yc, yd, x, g1, sc2, sh2, n2g, wo_bf, wr_hi, wr_lo, br, tm):
    bn, sn, d = x.shape
    g = GROUP_W
    yspec = pl.BlockSpec((1, tm, g), lambda b, s: (b, s, 0))
    vec_spec = pl.BlockSpec((1, 1, d), lambda b, s: (b, 0, 0))
    xspec = pl.BlockSpec((1, tm, d), lambda b, s: (b, s, 0))
    return pl.pallas_call(
        _out_kernel,
        grid=(bn, sn // tm),
        in_specs=[yspec, yspec, yspec, yspec, xspec, vec_spec, vec_spec, vec_spec,
                  pl.BlockSpec((1, d), lambda b, s: (0, 0)),
                  pl.BlockSpec((d, d), lambda b, s: (0, 0)),
                  pl.BlockSpec((d, GATE_LANES), lambda b, s: (0, 0)),
                  pl.BlockSpec((d, GATE_LANES), lambda b, s: (0, 0)),
                  pl.BlockSpec((1, GATE_LANES), lambda b, s: (0, 0))],
        out_specs=[xspec, xspec, pl.BlockSpec((1, tm, GATE_LANES), lambda b, s: (b, s, 0))],
        out_shape=[jax.ShapeDtypeStruct((bn, sn, d), F32),
                   jax.ShapeDtypeStruct((bn, sn, d), BF16),
                   jax.ShapeDtypeStruct((bn, sn, GATE_LANES), F32)],
        compiler_params=_cparams("parallel", "arbitrary"),
        name="out_proj",
    )(ya, yb, yc, yd, x, g1, sc2, sh2, n2g, wo_bf, wr_hi, wr_lo, br)


def _mask_bf16(m):
    return jnp.where(m, 1.0, 0.0).astype(BF16)


def _start_copies(tab_ref, cnt_s, slot, make_copy):
    nc = len(COPY_CHUNKS)
    for k, chunks in enumerate(COPY_CHUNKS):
        cnt_s[slot * nc + k] = tab_ref[0, 0, COUNT_BASE + k]

        def issue(j, carry, k=k, chunks=chunks):
            v = tab_ref[0, 0, COPY_BASE[k] + j]
            local_row = pl.multiple_of((v & 255) * CHUNK_ROWS, CHUNK_ROWS)
            global_row = pl.multiple_of((v >> 8) * CHUNK_ROWS, CHUNK_ROWS)
            make_copy(chunks * CHUNK_ROWS, local_row, global_row).start()
            return carry
        lax.fori_loop(0, cnt_s[slot * nc + k], issue, 0)


def _wait_copies(cnt_s, slot, make_copy):
    nc = len(COPY_CHUNKS)
    for k, chunks in enumerate(COPY_CHUNKS):
        def wait_one(j, carry, chunks=chunks):
            make_copy(chunks * CHUNK_ROWS, 0, 0).wait()
            return carry
        lax.fori_loop(0, cnt_s[slot * nc + k], wait_one, 0)


def _block_rows(tab_ref):
    return CHUNK_ROWS * sum(chunks * tab_ref[0, 0, COUNT_BASE + k] for k, chunks in enumerate(COPY_CHUNKS))


def _dispatch_kernel(tab_ref, h2_ref, g_ref, *rest):
    xs_hbm, buf, sem, cnt_s = rest[-4:]
    b = pl.program_id(0)
    nb = pl.num_programs(0)
    slot = b % 2
    ch = CHUNK_ROWS
    tb = TOKEN_BLOCK
    gl = GATE_LANES

    def row_copy(s, rows, local_row, global_row):
        return pltpu.make_async_copy(buf.at[s, pl.ds(local_row, rows), :], xs_hbm.at[pl.ds(global_row, rows), :],
                                     sem.at[s])

    @pl.when(b >= 2)
    def _():
        _wait_copies(cnt_s, slot, functools.partial(row_copy, slot))

    mt = g_ref[...].T > 0.0
    mt_bf = _mask_bf16(mt)
    ti = lax.broadcasted_iota(jnp.int32, (tb, tb), 0)
    tj = lax.broadcasted_iota(jnp.int32, (tb, tb), 1)
    rank = jnp.dot(mt_bf, _mask_bf16(ti < tj), preferred_element_type=F32)
    cnt = jnp.sum(jnp.where(mt, 1.0, 0.0), axis=1, keepdims=True)
    nch = jnp.floor((cnt + (ch - 1.0)) * (1.0 / ch))
    ei = lax.broadcasted_iota(jnp.int32, (gl, gl), 0)
    ej = lax.broadcasted_iota(jnp.int32, (gl, gl), 1)
    loc = ch * jnp.dot(_mask_bf16(ej < ei), jnp.broadcast_to(nch, (gl, gl)).astype(BF16),
                       preferred_element_type=F32)[:, 0:1]
    order = jnp.dot(_mask_bf16(ej <= ei), mt_bf, preferred_element_type=F32)
    dest = loc + rank
    dks = []
    for k in range(TOP_K):
        selk = mt & (order == float(k + 1))
        dk = jnp.sum(jnp.where(selk, dest + 1.0, 0.0), axis=0, keepdims=True) - 1.0
        dks.append(dk.astype(jnp.int32))
    h2 = h2_ref[...]
    for rg in range(LOCAL_ROWS // ROW_GROUP):
        ri = lax.broadcasted_iota(jnp.int32, (ROW_GROUP, tb), 0) + rg * ROW_GROUP
        hit = (ri == dks[0]) | (ri == dks[1]) | (ri == dks[2]) | (ri == dks[3])
        srt = jnp.dot(_mask_bf16(hit), h2, preferred_element_type=F32)
        buf[slot, rg * ROW_GROUP:(rg + 1) * ROW_GROUP, :] = srt.astype(BF16)

    _start_copies(tab_ref, cnt_s, slot, functools.partial(row_copy, slot))

    @pl.when(b == nb - 1)
    def _():
        _wait_copies(cnt_s, slot, functools.partial(row_copy, slot))

        @pl.when(b >= 1)
        def _():
            _wait_copies(cnt_s, 1 - slot, functools.partial(row_copy, 1 - slot))


def _dispatch(table, blk0, h2, gates, rtot, xs_prev=None):
    n, d = h2.shape
    nb = n // TOKEN_BLOCK
    in_specs = [pl.BlockSpec((1, 1, TABLE_W), lambda b: (b + blk0, 0, 0), memory_space=pltpu.SMEM),
                pl.BlockSpec((TOKEN_BLOCK, d), lambda b: (b, 0)),
                pl.BlockSpec((TOKEN_BLOCK, GATE_LANES), lambda b: (b, 0))]
    args = [table, h2, gates]
    aliases = {}
    if xs_prev is not None:
        in_specs.append(pl.BlockSpec(memory_space=pl.ANY))
        args.append(xs_prev)
        aliases = {3: 0}
    return pl.pallas_call(
        _dispatch_kernel,
        grid=(nb,),
        in_specs=in_specs,
        out_specs=pl.BlockSpec(memory_space=pl.ANY),
        out_shape=jax.ShapeDtypeStruct((rtot, d), BF16),
        scratch_shapes=[pltpu.VMEM((2, LOCAL_ROWS, d), BF16),
                        pltpu.SemaphoreType.DMA((2,)),
                        pltpu.SMEM((2 * len(COPY_CHUNKS),), jnp.int32)],
        input_output_aliases=aliases,
        compiler_params=_cparams("arbitrary"),
        name="moe_dispatch",
    )(*args)


def _combine_kernel(tab_ref, tabn_ref, x_ref, g2_ref, g_ref, y_hbm, o_ref, ybuf, sem, cnt_s):
    ch = CHUNK_ROWS
    tb = TOKEN_BLOCK
    gl = GATE_LANES
    b = pl.program_id(0)
    slot = b % 2
    n_rows = _block_rows(tab_ref)

    def row_copy(s, rows, local_row, global_row):
        return pltpu.make_async_copy(y_hbm.at[pl.ds(global_row, rows), :], ybuf.at[s, pl.ds(local_row, rows), :],
                                     sem.at[s])

    @pl.when(b == 0)
    def _():
        _start_copies(tab_ref, cnt_s, slot, functools.partial(row_copy, slot))

    _start_copies(tabn_ref, cnt_s, 1 - slot, functools.partial(row_copy, 1 - slot))

    g = g_ref[...]
    m = g > 0.0
    m_bf = _mask_bf16(m)
    ti = lax.broadcasted_iota(jnp.int32, (tb, tb), 0)
    tj = lax.broadcasted_iota(jnp.int32, (tb, tb), 1)
    rank = jnp.dot(_mask_bf16(tj < ti), m_bf, preferred_element_type=F32)
    cnt = jnp.sum(jnp.where(m, 1.0, 0.0), axis=0, keepdims=True)
    nch = jnp.floor((cnt + (ch - 1.0)) * (1.0 / ch))
    ei = lax.broadcasted_iota(jnp.int32, (gl, gl), 0)
    ej = lax.broadcasted_iota(jnp.int32, (gl, gl), 1)
    loc = ch * jnp.dot(jnp.broadcast_to(nch, (8, gl)).astype(BF16), _mask_bf16(ei < ej),
                       preferred_element_type=F32)[0:1, :]
    order = jnp.dot(m_bf, _mask_bf16(ei <= ej), preferred_element_type=F32)
    dest = loc + rank
    dks, gks = [], []
    for k in range(TOP_K):
        selk = m & (order == float(k + 1))
        dk = jnp.sum(jnp.where(selk, dest + 1.0, 0.0), axis=1, keepdims=True) - 1.0
        dks.append(dk.astype(jnp.int32))
        gks.append(jnp.sum(jnp.where(selk, g, 0.0), axis=1, keepdims=True))

    _wait_copies(cnt_s, slot, functools.partial(row_copy, slot))

    acc = jnp.zeros(x_ref.shape, F32)
    for rg in range(LOCAL_ROWS // ROW_GROUP):
        li = lax.broadcasted_iota(jnp.int32, (tb, ROW_GROUP), 1) + rg * ROW_GROUP
        cmat = jnp.zeros((tb, ROW_GROUP), F32)
        for dk, gk in zip(dks, gks):
            cmat = cmat + jnp.where(li == dk, gk, 0.0)
        ri = lax.broadcasted_iota(jnp.int32, (ROW_GROUP, 1), 0) + rg * ROW_GROUP
        yv = ybuf[slot, rg * ROW_GROUP:(rg + 1) * ROW_GROUP, :]
        yv = jnp.where(ri < n_rows, yv, jnp.zeros_like(yv))
        acc = acc + jnp.dot(cmat.astype(BF16), yv, preferred_element_type=F32)
    o_ref[...] = x_ref[...] + g2_ref[0] * acc


def _combine(table, blk0, x, g2, gates, ys):
    bn, sn, d = x.shape
    n = bn * sn
    per_seq = sn // TOKEN_BLOCK
    n_blocks = n // TOKEN_BLOCK
    empty_row = table.shape[0] - 1
    out = pl.pallas_call(
        _combine_kernel,
        grid=(n_blocks,),
        in_specs=[pl.BlockSpec((1, 1, TABLE_W), lambda b: (b + blk0, 0, 0), memory_space=pltpu.SMEM),
                  pl.BlockSpec((1, 1, TABLE_W),
                               lambda b: (jnp.where(b + 1 < n_blocks, b + blk0 + 1, empty_row), 0, 0),
                               memory_space=pltpu.SMEM),
                  pl.BlockSpec((TOKEN_BLOCK, d), lambda b: (b, 0)),
                  pl.BlockSpec((1, 1, d), lambda b: (b // per_seq, 0, 0)),
                  pl.BlockSpec((TOKEN_BLOCK, GATE_LANES), lambda b: (b, 0)),
                  pl.BlockSpec(memory_space=pl.ANY)],
        out_specs=pl.BlockSpec((TOKEN_BLOCK, d), lambda b: (b, 0)),
        out_shape=jax.ShapeDtypeStruct((n, d), F32),
        scratch_shapes=[pltpu.VMEM((2, LOCAL_ROWS, d), BF16),
                        pltpu.SemaphoreType.DMA((2,)),
                        pltpu.SMEM((2 * len(COPY_CHUNKS),), jnp.int32)],
        compiler_params=_cparams("arbitrary"),
        name="moe_combine",
    )(table, table, x.reshape(n, d), g2, gates.reshape(n, GATE_LANES), ys)
    return out.reshape(bn, sn, d)


TILE_ACTIVE, TILE_FIRST, TILE_SLOT, TILE_COLD = 1, 2, 4, 8


def _expert_kernel(te_ref, tf_ref, nv_ref, nx_ref, x_ref, wgu_hbm, bgu_ref, wdn_hbm, bdn_ref, y_ref,
                   wgu_st, wdn_st, wgu_s, wdn_s, sem, *, layer):
    i = pl.program_id(0)
    flag = tf_ref[i]
    d, f2 = wgu_s.shape
    f = f2 // 2
    rows = 128

    def weight_copies(e, s):
        return (pltpu.make_async_copy(wgu_hbm.at[layer, e], wgu_st.at[s], sem.at[0, s]),
                pltpu.make_async_copy(wdn_hbm.at[layer, e], wdn_st.at[s], sem.at[1, s]))

    @pl.when((flag & TILE_FIRST) != 0)
    def _():
        slot = (flag // TILE_SLOT) & 1

        @pl.when((flag & TILE_COLD) != 0)
        def _():
            for cp in weight_copies(te_ref[i], slot):
                cp.start()

        for cp in weight_copies(te_ref[i], slot):
            cp.wait()
        nxt = nx_ref[i]

        @pl.when(nxt >= 0)
        def _():
            for cp in weight_copies(nxt, 1 - slot):
                cp.start()

        def cast_gu(r, c):
            r0 = pl.multiple_of(r * rows, rows)
            wgu_s[pl.ds(r0, rows), :] = wgu_st[slot, pl.ds(r0, rows), :].astype(BF16)
            return c
        lax.fori_loop(0, d // rows, cast_gu, 0)

        def cast_dn(r, c):
            r0 = pl.multiple_of(r * rows, rows)
            wdn_s[pl.ds(r0, rows), :] = wdn_st[slot, pl.ds(r0, rows), :].astype(BF16)
            return c
        lax.fori_loop(0, f // rows, cast_dn, 0)

    @pl.when((flag & TILE_ACTIVE) != 0)
    def _():
        ri = lax.broadcasted_iota(jnp.int32, (x_ref.shape[0], 1), 0)
        x = x_ref[...]
        x = jnp.where(ri < nv_ref[i], x, jnp.zeros_like(x))
        gu = jnp.dot(x, wgu_s[...], preferred_element_type=F32) + bgu_ref[0]
        gl = jnp.minimum(gu[:, :f], SWIGLU_LIMIT)
        up = jnp.clip(gu[:, f:], -SWIGLU_LIMIT, SWIGLU_LIMIT)
        hid = (up + 1.0) * gl * jax.nn.sigmoid(SWIGLU_ALPHA * gl)
        y = jnp.dot(hid.astype(BF16), wdn_s[...], preferred_element_type=F32) + bdn_ref[0]
        y_ref[...] = y.astype(BF16)

    @pl.when((flag & TILE_ACTIVE) == 0)
    def _():
        y_ref[...] = jnp.zeros_like(y_ref)


def _experts(tile_expert, tile_flag, tile_rows, tile_next, xs, w_gu, b_gu, w_dn, b_dn, layer, tm):
    rtot, d = xs.shape
    _, ne, _, f2 = w_gu.shape
    f = f2 // 2
    grid_spec = pltpu.PrefetchScalarGridSpec(
        num_scalar_prefetch=4,
        grid=(rtot // tm,),
        in_specs=[pl.BlockSpec((tm, d), lambda i, te, tf, nv, nx: (i, 0)),
                  pl.BlockSpec(memory_space=pl.ANY),
                  pl.BlockSpec((1, 1, f2), lambda i, te, tf, nv, nx: (te[i], 0, 0)),
                  pl.BlockSpec(memory_space=pl.ANY),
                  pl.BlockSpec((1, 1, d), lambda i, te, tf, nv, nx: (te[i], 0, 0))],
        out_specs=pl.BlockSpec((tm, d), lambda i, te, tf, nv, nx: (i, 0)),
        scratch_shapes=[pltpu.VMEM((2, d, f2), F32), pltpu.VMEM((2, f, d), F32),
                        pltpu.VMEM((d, f2), BF16), pltpu.VMEM((f, d), BF16),
                        pltpu.SemaphoreType.DMA((2, 2))],
    )
    return pl.pallas_call(
        functools.partial(_expert_kernel, layer=layer),
        grid_spec=grid_spec,
        out_shape=jax.ShapeDtypeStruct((rtot, d), BF16),
        compiler_params=_cparams("arbitrary"),
        name="experts",
    )(tile_expert, tile_flag, tile_rows, tile_next, xs, w_gu, b_gu.reshape(ne, 1, f2), w_dn,
      b_dn.reshape(ne, 1, d))


def _moe_layout(gate_streams, tm):
    n = sum(gs.shape[0] for gs in gate_streams)
    nb = n // TOKEN_BLOCK
    ne = N_EXPERTS
    ch = CHUNK_ROWS
    tmc = tm // ch
    n_tiles = -(-(n * TOP_K + nb * ne * (ch - 1) + ne * (tm - ch)) // tm)
    i32 = jnp.int32
    cnt = jnp.concatenate(
        [jnp.sum((gs[:, :ne] > 0.0).astype(i32).reshape(-1, TOKEN_BLOCK, ne), axis=1) for gs in gate_streams], axis=0)
    nch = (cnt + ch - 1) // ch
    loc_end = jnp.cumsum(nch, axis=1)
    loc_off = loc_end - nch
    exp_ch = jnp.sum(nch, axis=0)
    exp_pad = ((exp_ch + tmc - 1) // tmc) * tmc
    exp_end = jnp.cumsum(exp_pad)
    exp_off = exp_end - exp_pad
    seg_off = exp_off[None, :] + jnp.cumsum(nch, axis=0) - nch
    per_seg = (nch // 4, (nch % 4) // 2, nch % 2)
    seg_start = (0 * nch, 4 * per_seg[0], 4 * per_seg[0] + 2 * per_seg[1])
    sections, totals = [], []
    for cnt_k, start_k, chunks, slots in zip(per_seg, seg_start, COPY_CHUNKS, COPY_SLOTS):
        cum = jnp.cumsum(cnt_k, axis=1)
        j = jnp.arange(slots, dtype=i32)
        e_of_j = jnp.minimum(jnp.sum((cum[:, None, :] <= j[None, :, None]).astype(i32), axis=2), ne - 1)
        hot = (e_of_j[:, :, None] == jnp.arange(ne, dtype=i32)).astype(i32)
        within = chunks * (j[None, :] - jnp.sum(hot * (cum - cnt_k)[:, None, :], axis=2))
        local = jnp.sum(hot * (loc_off + start_k)[:, None, :], axis=2) + within
        glob = jnp.sum(hot * (seg_off + start_k)[:, None, :], axis=2) + within
        sections.append(jnp.where(j[None, :] < cum[:, ne - 1:], glob * 256 + local, 0))
        totals.append(cum[:, ne - 1:])
    pad = jnp.zeros((nb, COUNT_BASE - sum(COPY_SLOTS)), i32)
    table = jnp.concatenate(sections + [pad] + totals, axis=1)
    table = jnp.concatenate([table, jnp.zeros((1, TABLE_W), i32)], axis=0)
    tile_start = jnp.arange(n_tiles, dtype=i32) * tmc
    te = jnp.minimum(jnp.sum((exp_end[None, :] <= tile_start[:, None]).astype(i32), axis=1), ne - 1)
    te_hot = (te[:, None] == jnp.arange(ne, dtype=i32)).astype(i32)
    t_off = jnp.sum(te_hot * exp_off[None, :], axis=1)
    t_len = jnp.sum(te_hot * exp_ch[None, :], axis=1)
    total = exp_end[ne - 1]
    active = tile_start < total
    first = active & (tile_start == t_off)
    tile_rows = jnp.clip((t_len - (tile_start - t_off)) * ch, 0, tm)
    last_active = (tile_start + tmc == total).astype(i32)
    te = jnp.where(active, te, jnp.sum(last_active * te))
    group = jnp.cumsum(first.astype(i32)) - 1
    ei = jnp.arange(ne, dtype=i32)
    later = (ei[None, :] > ei[:, None]) & (exp_ch[None, :] > 0)
    next_e = jnp.min(jnp.where(later, ei[None, :], ne), axis=1)
    next_e = jnp.where(next_e < ne, next_e, -1)
    tile_next = jnp.sum(te_hot * next_e[None, :], axis=1)
    tile_flag = (TILE_ACTIVE * active.astype(i32) + TILE_FIRST * first.astype(i32) + TILE_SLOT * (group & 1)
                 + TILE_COLD * (first & (group == 0)).astype(i32))
    return table.reshape(nb + 1, 1, TABLE_W), (te, tile_flag, tile_rows, tile_next), n_tiles * tm


def _rope_tables(sn, scale_q, roped):
    if not roped:
        ones = jnp.ones((sn, GROUP_W), F32)
        zeros = jnp.zeros((sn, GROUP_W), F32)
        return ones * scale_q, zeros, ones, zeros
    pos = jnp.arange(sn)
    row = (pos // GRID_W).astype(F32)
    col = (pos % GRID_W).astype(F32)
    nf = HEAD_QK_B // 4
    inv = ROPE_BASE ** (-jnp.arange(nf, dtype=F32) / nf)
    ar = row[:, None] * inv
    ac = col[:, None] * inv
    cos32 = jnp.concatenate([jnp.cos(ar), jnp.cos(ar), jnp.cos(ac), jnp.cos(ac)], axis=1)
    sin32 = jnp.concatenate([-jnp.sin(ar), jnp.sin(ar), -jnp.sin(ac), jnp.sin(ac)], axis=1)
    reps = GROUP_W // HEAD_QK_B
    cos = jnp.tile(cos32, (1, reps))
    sin = jnp.tile(sin32, (1, reps))
    return cos * scale_q, sin * scale_q, cos, sin


def kernel(x, c, ctx, c_ctx, w_mod, b_mod, norm1_g, norm2_g, w_in, a_vnorm_g, a_ws, a_bs, b_qnorm_g, b_knorm_g, b_lam_q1, b_lam_k1, b_lam_q2, b_lam_k2, b_subln_g, c_conv_w, c_conv_b, c_ln_g, c_ln_b, d_conv_w, w_out, router_w, router_b, exp_w_gu, exp_b_gu, exp_w_dn, exp_b_dn):
    bsz, seq, dm = x.shape
    ctx_len = ctx.shape[1]
    depth = w_in.shape[0]
    g = GROUP_W
    tm_lat = min(512, seq)
    tm_ctx = min(512, ctx_len)
    tq_lat = min(512, seq)
    tq_ctx = min(256, ctx_len)
    tm_moe = 512
    assert seq % TOKEN_BLOCK == 0 and ctx_len % TOKEN_BLOCK == 0

    n_mod_rows = -(-(bsz + 1) // 8) * 8
    cc = jnp.concatenate([c, c_ctx[None, :], jnp.zeros((n_mod_rows - bsz - 1, dm), F32)], axis=0)
    mods = _modulation(cc, w_mod, b_mod)

    gsum32 = _group_ones(g, HEAD_QK_B)
    gsum64 = _group_ones(g, HEAD_V_B)
    scale_q = HEAD_QK_B ** -0.5 * math.log2(math.e)
    tabs_lat = _rope_tables(seq, scale_q, True)
    tabs_ctx = _rope_tables(ctx_len, scale_q, False)

    xc = ctx
    for l in range(depth):
        last = l == depth - 1
        lam_init = 0.8 - 0.6 * math.exp(-0.3 * l)
        lam = (jnp.exp(jnp.sum(b_lam_q1[l] * b_lam_k1[l])) - jnp.exp(jnp.sum(b_lam_q2[l] * b_lam_k2[l]))
               + lam_init).astype(F32)
        score_bound = 1.01 * scale_q * HEAD_QK_B * jnp.max(jnp.abs(b_qnorm_g[l])) * jnp.max(jnp.abs(b_knorm_g[l]))
        lam = jnp.stack([lam, score_bound.astype(F32)])
        m_lat = mods[l, :bsz].reshape(bsz, 6, 1, dm)
        m_ctx = jnp.broadcast_to(mods[l, bsz].reshape(1, 6, 1, dm), (bsz, 6, 1, dm))
        sh1, sc1, g1, sh2, sc2, g2 = [m_lat[:, i] for i in range(6)]
        sh1c, sc1c, g1c, sh2c, sc2c, g2c = [m_ctx[:, i] for i in range(6)]

        w_in_bf = w_in[l].astype(BF16)
        n1g = norm1_g[l][None, :]
        n2g = norm2_g[l][None, :]
        reps = g // HEAD_QK_B // 2
        qg = jnp.tile(b_qnorm_g[l].reshape(1, 2 * HEAD_QK_B), (1, reps))
        kg = jnp.tile(b_knorm_g[l].reshape(1, 2 * HEAD_QK_B), (1, reps))
        gpost = jnp.tile(b_subln_g[l][None, :], (1, N_HEADS_B)) * (1.0 - lam_init)
        vg = a_vnorm_g[l][None, :]
        ws_bf = a_ws[l].astype(BF16)
        bs_exp = jnp.repeat(a_bs[l].T, HEAD_A, axis=1)
        mix_params = (vg, ws_bf, bs_exp, gsum64, c_conv_w[l], c_conv_b[l][None, :], c_ln_g[l][None, :],
                      c_ln_b[l][None, :], d_conv_w[l])
        wo_bf = w_out[l].astype(BF16)
        wr = jnp.pad(router_w[l], ((0, 0), (0, GATE_LANES - N_EXPERTS)))
        wr_hi = wr.astype(BF16)
        wr_lo = (wr - wr_hi.astype(F32)).astype(BF16)
        br = jnp.pad(router_b[l], (0, GATE_LANES - N_EXPERTS))[None, :]

        p_lat = _in_proj(x, sc1, sh1, n1g, w_in_bf, gsum32, qg, kg, tabs_lat, tm_lat)
        p_ctx = _in_proj(xc, sc1c, sh1c, n1g, w_in_bf, gsum32, qg, kg, tabs_ctx, tm_ctx)

        yb = _attention(lam, p_lat, (p_lat, p_ctx), gpost, gsum64, tq_lat)
        ya, yc, yd = _mixers(p_lat, *mix_params)
        x, h2, gates = _out_proj(ya, yb, yc, yd, x, g1, sc2, sh2, n2g, wo_bf, wr_hi, wr_lo, br, tm_lat)
        moe_w = (exp_w_gu, exp_b_gu[l], exp_w_dn, exp_b_dn[l], l, tm_moe)
        h2f = h2.reshape(-1, dm)
        gatesf = gates.reshape(-1, GATE_LANES)
        if last:
            table, tiles, rtot = _moe_layout((gatesf,), tm_moe)
            xs = _dispatch(table, 0, h2f, gatesf, rtot)
            ys = _experts(*tiles, xs, *moe_w)
            x = _combine(table, 0, x, g2, gates, ys)
        else:
            ybc = _attention(lam, p_ctx, (p_ctx,), gpost, gsum64, tq_ctx)
            yac, ycc, ydc = _mixers(p_ctx, *mix_params)
            xc, h2c, gatesc = _out_proj(yac, ybc, ycc, ydc, xc, g1c, sc2c, sh2c, n2g, wo_bf, wr_hi, wr_lo, br,
                                        tm_ctx)
            gatescf = gatesc.reshape(-1, GATE_LANES)
            blk_ctx = bsz * seq // TOKEN_BLOCK
            table, tiles, rtot = _moe_layout((gatesf, gatescf), tm_moe)
            xs = _dispatch(table, 0, h2f, gatesf, rtot)
            xs = _dispatch(table, blk_ctx, h2c.reshape(-1, dm), gatescf, rtot, xs_prev=xs)
            ys = _experts(*tiles, xs, *moe_w)
            x = _combine(table, 0, x, g2, gates, ys)
            xc = _combine(table, blk_ctx, xc, g2c, gatesc, ys)
    return x
```

```python
import functools
import math

import jax
import jax.numpy as jnp
from jax import lax
from jax.experimental import pallas as pl
from jax.experimental.pallas import tpu as pltpu

F32 = jnp.float32
BF16 = jnp.bfloat16

GROUP_W = 256
N_SPLITS = 10
N_HEADS_A = 4
HEAD_A = 64
CHUNK = 128
N_HEADS_B = 4
HEAD_V_B = 64
HEAD_QK_B = 32
GRID_W = 64
ROPE_BASE = 10000.0
CONV_C = 31
CONV_D = 3
N_EXPERTS = 32
TOP_K = 4
SWIGLU_LIMIT = 7.0
SWIGLU_ALPHA = 1.702
EPS = 1e-6

GATE_LANES = 128
CONV_HALO = 16
CONV_SPAN = CHUNK + 8 * ((CONV_HALO - CONV_C // 2 + CONV_C - 1) // 8)
VMEM_LIMIT = 56 * 1024 * 1024
MAX_FIXED_SHIFT = 60.0
ATTN_LOOKAHEAD = 2

TOKEN_BLOCK = 256
CHUNK_ROWS = 16
ROW_GROUP = 256
LOCAL_ROWS = -(-(TOKEN_BLOCK * TOP_K + N_EXPERTS * (CHUNK_ROWS - 1)) // ROW_GROUP) * ROW_GROUP
MAX_CHUNKS = LOCAL_ROWS // CHUNK_ROWS
COPY_CHUNKS = (4, 2, 1)
COPY_SLOTS = (MAX_CHUNKS // 4, N_EXPERTS, N_EXPERTS)
COPY_BASE = (0, COPY_SLOTS[0], COPY_SLOTS[0] + COPY_SLOTS[1])
TABLE_W = 128
COUNT_BASE = TABLE_W - len(COPY_CHUNKS)
assert MAX_CHUNKS <= 256 and sum(COPY_SLOTS) <= COUNT_BASE


def _cparams(*sem):
    return pltpu.CompilerParams(dimension_semantics=sem, vmem_limit_bytes=VMEM_LIMIT)


def _group_ones(width, group):
    i = jnp.arange(width) // group
    return (i[:, None] == i[None, :]).astype(BF16)


def _mod_kernel(c_ref, w_ref, b_ref, o_ref):
    c = c_ref[...]
    a = (c * jax.nn.sigmoid(c)).astype(BF16)
    o_ref[0] = jnp.dot(a, w_ref[0].astype(BF16), preferred_element_type=F32) + b_ref[0]


def _modulation(cc, w_mod, b_mod):
    nl, d, d6 = w_mod.shape
    r = cc.shape[0]
    return pl.pallas_call(
        _mod_kernel,
        grid=(nl, d6 // d),
        in_specs=[pl.BlockSpec((r, d), lambda l, j: (0, 0)),
                  pl.BlockSpec((1, d, d), lambda l, j: (l, 0, j)),
                  pl.BlockSpec((1, 1, d), lambda l, j: (l, 0, j))],
        out_specs=pl.BlockSpec((1, r, d), lambda l, j: (l, 0, j)),
        out_shape=jax.ShapeDtypeStruct((nl, r, d6), F32),
        compiler_params=_cparams("arbitrary", "arbitrary"),
        name="modulation",
    )(cc, w_mod, b_mod.reshape(nl, 1, d6))


def _rope_norm(t, gsum, gain, cos, sin, lo_lane):
    ss = jnp.dot((t * t).astype(BF16), gsum, preferred_element_type=F32)
    tn = t * lax.rsqrt(ss * (1.0 / HEAD_QK_B) + EPS) * gain
    rot = jnp.where(lo_lane, pltpu.roll(tn, GROUP_W - 8, 1), pltpu.roll(tn, 8, 1))
    return tn * cos + rot * sin


def _in_kernel(x_ref, sc_ref, sh_ref, g_ref, w_ref, gsum_ref, qg_ref, kg_ref,
               cq_ref, sq_ref, ck_ref, sk_ref, o_ref):
    g = GROUP_W
    tm = x_ref.shape[1]
    n_parts = 2 if tm % 16 == 0 else 1
    rows = tm // n_parts
    lane = lax.broadcasted_iota(jnp.int32, (1, g), 1)
    lo_lane = (lane & 8) == 0
    gsum = gsum_ref[...]

    def project(r0):
        x = x_ref[0, r0:r0 + rows, :]
        ms = jnp.mean(x * x, axis=-1, keepdims=True)
        h = (x * lax.rsqrt(ms + EPS) * g_ref[...]) * (1.0 + sc_ref[0]) + sh_ref[0]
        return jnp.dot(h.astype(BF16), w_ref[...], preferred_element_type=F32)

    def finish(r0, p):
        sl = slice(r0, r0 + rows)
        q = _rope_norm(p[:, 2 * g:3 * g], gsum, qg_ref[...], cq_ref[sl, :], sq_ref[sl, :], lo_lane)
        k = _rope_norm(p[:, 3 * g:4 * g], gsum, kg_ref[...], ck_ref[sl, :], sk_ref[sl, :], lo_lane)
        o_ref[0, sl, 0:2 * g] = p[:, 0:2 * g].astype(BF16)
        o_ref[0, sl, 2 * g:3 * g] = q.astype(BF16)
        o_ref[0, sl, 3 * g:4 * g] = k.astype(BF16)
        o_ref[0, sl, 4 * g:] = p[:, 4 * g:].astype(BF16)

    ps = [project(i * rows) for i in range(n_parts)]
    for i, p in enumerate(ps):
        finish(i * rows, p)


def _in_proj(x, sc, sh, g, w_bf, gsum32, qg, kg, tabs, tm):
    bn, sn, d = x.shape
    d_in = w_bf.shape[1]
    tab_spec = pl.BlockSpec((tm, GROUP_W), lambda b, s: (s, 0))
    vec_spec = pl.BlockSpec((1, 1, d), lambda b, s: (b, 0, 0))
    row256 = pl.BlockSpec((1, GROUP_W), lambda b, s: (0, 0))
    return pl.pallas_call(
        _in_kernel,
        grid=(bn, sn // tm),
        in_specs=[pl.BlockSpec((1, tm, d), lambda b, s: (b, s, 0)),
                  vec_spec, vec_spec,
                  pl.BlockSpec((1, d), lambda b, s: (0, 0)),
                  pl.BlockSpec((d, d_in), lambda b, s: (0, 0)),
                  pl.BlockSpec((GROUP_W, GROUP_W), lambda b, s: (0, 0)),
                  row256, row256, tab_spec, tab_spec, tab_spec, tab_spec],
        out_specs=pl.BlockSpec((1, tm, d_in), lambda b, s: (b, s, 0)),
        out_shape=jax.ShapeDtypeStruct((bn, sn, d_in), BF16),
        compiler_params=_cparams("parallel", "arbitrary"),
        name="in_proj",
    )(x, sc, sh, g, w_bf, gsum32, qg, kg, *tabs)


def _mix_kernel(au_ref, av_ref, ca_ref, cg_ref, db_ref, dc_ref, dh_ref,
                vg_ref, ws_ref, bs_ref, gsum_ref, cw_ref, cb_ref, lng_ref, lnb_ref, dw_ref,
                ya_ref, yc_ref, yd_ref, zc_ref, zd_ref, zs_ref, *, sn):
    g = GROUP_W
    lane = lax.broadcasted_iota(jnp.int32, (1, g), 1)
    n_chunks = sn // CHUNK

    def gmlp_chunk(ci, carry):
        r0 = pl.multiple_of(ci * CHUNK, CHUNK)
        u = jax.nn.gelu(au_ref[0, pl.ds(r0, CHUNK), :].astype(F32))
        v = jax.nn.gelu(av_ref[0, pl.ds(r0, CHUNK), :].astype(F32))
        ss = jnp.dot((v * v).astype(BF16), gsum_ref[...], preferred_element_type=F32)
        vn = (v * lax.rsqrt(ss * (1.0 / HEAD_A) + EPS) * vg_ref[...]).astype(BF16)
        mixed = bs_ref[...]
        for h in range(N_HEADS_A):
            mh = jnp.dot(ws_ref[h], vn, preferred_element_type=F32)
            hm = (lane >= h * HEAD_A) & (lane < (h + 1) * HEAD_A)
            mixed = mixed + jnp.where(hm, mh, 0.0)
        ya_ref[0, pl.ds(r0, CHUNK), :] = (u * mixed).astype(BF16)
        return carry

    lax.fori_loop(0, n_chunks, gmlp_chunk, 0, unroll=2)

    zeros_halo = jnp.zeros((CONV_HALO, g), F32)
    zc_ref[0:CONV_HALO, :] = zeros_halo
    zc_ref[CONV_HALO + sn:CONV_HALO + sn + CONV_HALO, :] = zeros_halo
    zd_ref[0:CONV_HALO, :] = zeros_halo
    zd_ref[CONV_HALO + sn:CONV_HALO + sn + CONV_HALO, :] = zeros_halo

    def stage_chunk(ci, carry):
        r0 = pl.multiple_of(ci * CHUNK, CHUNK)
        ca = ca_ref[0, pl.ds(r0, CHUNK), :].astype(F32)
        cg = cg_ref[0, pl.ds(r0, CHUNK), :].astype(F32)
        zc_ref[pl.ds(r0 + CONV_HALO, CHUNK), :] = ca * jax.nn.sigmoid(cg)
        dc = dc_ref[0, pl.ds(r0, CHUNK), :].astype(F32)
        dh = dh_ref[0, pl.ds(r0, CHUNK), :].astype(F32)
        zd_ref[pl.ds(r0 + CONV_HALO, CHUNK), :] = dc * dh
        return carry

    lax.fori_loop(0, n_chunks, stage_chunk, 0)

    def conv_chunk(ci, carry):
        r0 = pl.multiple_of(ci * CHUNK, CHUNK)
        win = zc_ref[pl.ds(r0, CHUNK + 2 * CONV_HALO), :]
        acc = jnp.zeros((CHUNK, g), F32) + cb_ref[...]
        base = CONV_HALO - CONV_C // 2
        for ph in range(8):
            taps = [k for k in range(CONV_C) if (base + k) % 8 == ph]
            if taps:
                zs_ref[ph] = win[ph:ph + CONV_SPAN, :]
                for k in taps:
                    a = (base + k) // 8 * 8
                    acc = acc + cw_ref[k:k + 1, :] * zs_ref[ph, a:a + CHUNK, :]
        mu = jnp.mean(acc, axis=-1, keepdims=True)
        dev = acc - mu
        var = jnp.mean(dev * dev, axis=-1, keepdims=True)
        yln = dev * lax.rsqrt(var + EPS) * lng_ref[...] + lnb_ref[...]
        yc_ref[0, pl.ds(r0, CHUNK), :] = (yln * jax.nn.sigmoid(yln)).astype(BF16)

        wind = zd_ref[pl.ds(r0, CHUNK + 2 * CONV_HALO), :]
        accd = jnp.zeros((CHUNK, g), F32)
        based = CONV_HALO - CONV_D // 2
        for k in range(CONV_D):
            accd = accd + dw_ref[k:k + 1, :] * wind[based + k:based + k + CHUNK, :]
        db = db_ref[0, pl.ds(r0, CHUNK), :].astype(F32)
        yd_ref[0, pl.ds(r0, CHUNK), :] = (db * accd).astype(BF16)
        return carry

    lax.fori_loop(0, n_chunks, conv_chunk, 0)


def _mixers(p, vg, ws_bf, bs_exp, gsum64, cw, cb, lng, lnb, dw):
    bn, sn, _ = p.shape
    g = GROUP_W

    def col(j):
        return pl.BlockSpec((1, sn, g), lambda b, j=j: (b, 0, j))

    def full(a):
        nd = a.ndim
        return pl.BlockSpec(a.shape, lambda b, nd=nd: (0,) * nd)

    out_spec = pl.BlockSpec((1, sn, g), lambda b: (b, 0, 0))
    out_sds = jax.ShapeDtypeStruct((bn, sn, g), BF16)
    params = (vg, ws_bf, bs_exp, gsum64, cw, cb, lng, lnb, dw)
    return pl.pallas_call(
        functools.partial(_mix_kernel, sn=sn),
        grid=(bn,),
        in_specs=[col(0), col(1), col(5), col(6), col(7), col(8), col(9)] + [full(a) for a in params],
        out_specs=[out_spec, out_spec, out_spec],
        out_shape=[out_sds, out_sds, out_sds],
        scratch_shapes=[pltpu.VMEM((sn + 2 * CONV_HALO, g), F32),
                        pltpu.VMEM((sn + 2 * CONV_HALO, g), F32),
                        pltpu.VMEM((8, CONV_SPAN, g), F32)],
        compiler_params=_cparams("parallel"),
        name="mixers",
    )(p, p, p, p, p, p, p, *params)


def _attn_kernel(lam_ref, q_ref, *rest, n_seg):
    k_refs = rest[0:2 * n_seg:2]
    v_refs = rest[1:2 * n_seg:2]
    gpost_ref, gsum_ref, o_ref = rest[2 * n_seg:]
    lam = lam_ref[0]
    score_bound = lam_ref[1]
    q = q_ref[0]
    ks = [r[0] for r in k_refs]
    vs = [r[0] for r in v_refs]
    tq = q.shape[0]
    lane = lax.broadcasted_iota(jnp.int32, (1, GROUP_W), 1)

    def scores(u):
        lo = u * HEAD_QK_B
        qm = jnp.where((lane >= lo) & (lane < lo + HEAD_QK_B), q, jnp.zeros_like(q))
        return [lax.dot_general(qm, k, (((1,), (1,)), ((), ())), preferred_element_type=F32) for k in ks]

    def softmax_parts(ss, bounded):
        if bounded:
            mx = score_bound
        else:
            mx = jnp.max(ss[0], axis=-1, keepdims=True)
            for s in ss[1:]:
                mx = jnp.maximum(mx, jnp.max(s, axis=-1, keepdims=True))
        es = [jnp.exp2(s - mx) for s in ss]
        den = jnp.sum(es[0], axis=-1, keepdims=True)
        for e in es[1:]:
            den = den + jnp.sum(e, axis=-1, keepdims=True)
        return [e.astype(BF16) for e in es], den

    def attend(bounded):
        y = jnp.zeros((tq, GROUP_W), F32)
        n_units = 2 * N_HEADS_B
        pending = {u: scores(u) for u in range(min(ATTN_LOOKAHEAD, n_units))}
        parts = {}
        for u in range(n_units):
            if u + ATTN_LOOKAHEAD < n_units:
                pending[u + ATTN_LOOKAHEAD] = scores(u + ATTN_LOOKAHEAD)
            parts[u] = softmax_parts(pending.pop(u), bounded)
            if u % 2 == 1:
                h = u // 2
                (num1, den1), (num2, den2) = parts.pop(u - 1), parts.pop(u)
                coef = (lam * den1 / den2).astype(BF16)
                o = jnp.zeros((tq, GROUP_W), F32)
                for e1, e2, v in zip(num1, num2, vs):
                    o = o + jnp.dot(e1 - coef * e2, v, preferred_element_type=F32)
                o = o * (1.0 / den1)
                y = jnp.where((lane >= h * HEAD_V_B) & (lane < (h + 1) * HEAD_V_B), o, y)
        ss = jnp.dot((y * y).astype(BF16), gsum_ref[...], preferred_element_type=F32)
        o_ref[0] = (y * lax.rsqrt(ss * (1.0 / HEAD_V_B) + EPS) * gpost_ref[...]).astype(BF16)

    small = score_bound <= MAX_FIXED_SHIFT

    @pl.when(small)
    def _():
        attend(True)

    @pl.when(jnp.logical_not(small))
    def _():
        attend(False)


def _attention(lam, p_q, kv_sources, gpost, gsum64, tq):
    bn, sq, _ = p_q.shape
    g = GROUP_W
    kv_specs, kv_args = [], []
    for p in kv_sources:
        sk = p.shape[1]
        kv_specs += [pl.BlockSpec((1, sk, g), lambda b, i: (b, 0, 3)),
                     pl.BlockSpec((1, sk, g), lambda b, i: (b, 0, 4))]
        kv_args += [p, p]
    return pl.pallas_call(
        functools.partial(_attn_kernel, n_seg=len(kv_sources)),
        grid=(bn, sq // tq),
        in_specs=[pl.BlockSpec(memory_space=pltpu.SMEM),
                  pl.BlockSpec((1, tq, g), lambda b, i: (b, i, 2))] + kv_specs
                 + [pl.BlockSpec((1, g), lambda b, i: (0, 0)),
                    pl.BlockSpec((g, g), lambda b, i: (0, 0))],
        out_specs=pl.BlockSpec((1, tq, g), lambda b, i: (b, i, 0)),
        out_shape=jax.ShapeDtypeStruct((bn, sq, g), BF16),
        compiler_params=_cparams("parallel", "arbitrary"),
        name="diff_attention",
    )(lam, p_q, *kv_args, gpost, gsum64)


def _out_kernel(ya_ref, yb_ref, yc_ref, yd_ref, x_ref, g1_ref, sc2_ref, sh2_ref, n2g_ref,
                wo_ref, wrh_ref, wrl_ref, br_ref, xo_ref, h2_ref, gate_ref):
    g = GROUP_W
    tm = x_ref.shape[1]
    n_parts = 2 if tm % 16 == 0 else 1
    rows = tm // n_parts

    def project(r0):
        sl = slice(r0, r0 + rows)
        acc = jnp.dot(ya_ref[0, sl, :], wo_ref[0:g, :], preferred_element_type=F32)
        acc = acc + jnp.dot(yb_ref[0, sl, :], wo_ref[g:2 * g, :], preferred_element_type=F32)
        acc = acc + jnp.dot(yc_ref[0, sl, :], wo_ref[2 * g:3 * g, :], preferred_element_type=F32)
        return acc + jnp.dot(yd_ref[0, sl, :], wo_ref[3 * g:4 * g, :], preferred_element_type=F32)

    def finish(r0, acc):
        sl = slice(r0, r0 + rows)
        xn = x_ref[0, sl, :] + g1_ref[0] * acc
        xo_ref[0, sl, :] = xn
        ms = jnp.mean(xn * xn, axis=-1, keepdims=True)
        h2 = (xn * lax.rsqrt(ms + EPS) * n2g_ref[...]) * (1.0 + sc2_ref[0]) + sh2_ref[0]
        hh = h2.astype(BF16)
        h2_ref[0, sl, :] = hh
        hl = (h2 - hh.astype(F32)).astype(BF16)
        logits = (jnp.dot(hh, wrh_ref[...], preferred_element_type=F32)
                  + jnp.dot(hl, wrh_ref[...], preferred_element_type=F32)
                  + jnp.dot(hh, wrl_ref[...], preferred_element_type=F32)) + br_ref[...]
        lane = lax.broadcasted_iota(jnp.int32, (rows, GATE_LANES), 1)
        neg_inf = jnp.float32(-jnp.inf)
        l = jnp.where(lane < N_EXPERTS, logits, neg_inf)
        vals, sels = [], []
        for _ in range(TOP_K):
            m = jnp.max(l, axis=-1, keepdims=True)
            idx = jnp.min(jnp.where(l == m, lane, GATE_LANES), axis=-1, keepdims=True)
            sel = lane == idx
            vals.append(m)
            sels.append(sel)
            l = jnp.where(sel, neg_inf, l)
        es = [jnp.exp(vk - vals[0]) for vk in vals]
        inv = 1.0 / (es[0] + es[1] + es[2] + es[3])
        gates = jnp.zeros((rows, GATE_LANES), F32)
        for sel, ek in zip(sels, es):
            gates = jnp.where(sel, ek * inv, gates)
        gate_ref[0, sl, :] = gates

    accs = [project(p * rows) for p in range(n_parts)]
    for p, acc in enumerate(accs):
        finish(p * rows, acc)


def _out_proj(ya, yb, yc, yd, x, g1, sc2, sh2, n2g, wo_bf, wr_hi, wr_lo, br, tm):
    bn, sn, d = x.shape
    g = GROUP_W
    yspec = pl.BlockSpec((1, tm, g), lambda b, s: (b, s, 0))
    vec_spec = pl.BlockSpec((1, 1, d), lambda b, s: (b, 0, 0))
    xspec = pl.BlockSpec((1, tm, d), lambda b, s: (b, s, 0))
    return pl.pallas_call(
        _out_kernel,
        grid=(bn, sn // tm),
        in_specs=[yspec, yspec, yspec, yspec, xspec, vec_spec, vec_spec, vec_spec,
                  pl.BlockSpec((1, d), lambda b, s: (0, 0)),
                  pl.BlockSpec((d, d), lambda b, s: (0, 0)),
                  pl.BlockSpec((d, GATE_LANES), lambda b, s: (0, 0)),
                  pl.BlockSpec((d, GATE_LANES), lambda b, s: (0, 0)),
                  pl.BlockSpec((1, GATE_LANES), lambda b, s: (0, 0))],
        out_specs=[xspec, xspec, pl.BlockSpec((1, tm, GATE_LANES), lambda b, s: (b, s, 0))],
        out_shape=[jax.ShapeDtypeStruct((bn, sn, d), F32),
                   jax.ShapeDtypeStruct((bn, sn, d), BF16),
                   jax.ShapeDtypeStruct((bn, sn, GATE_LANES), F32)],
        compiler_params=_cparams("parallel", "arbitrary"),
        name="out_proj",
    )(ya, yb, yc, yd, x, g1, sc2, sh2, n2g, wo_bf, wr_hi, wr_lo, br)


def _mask_bf16(m):
    return jnp.where(m, 1.0, 0.0).astype(BF16)


def _start_copies(tab_ref, cnt_s, slot, make_copy):
    nc = len(COPY_CHUNKS)
    for k, chunks in enumerate(COPY_CHUNKS):
        cnt_s[slot * nc + k] = tab_ref[0, 0, COUNT_BASE + k]

        def issue(j, carry, k=k, chunks=chunks):
            v = tab_ref[0, 0, COPY_BASE[k] + j]
            local_row = pl.multiple_of((v & 255) * CHUNK_ROWS, CHUNK_ROWS)
            global_row = pl.multiple_of((v >> 8) * CHUNK_ROWS, CHUNK_ROWS)
            make_copy(chunks * CHUNK_ROWS, local_row, global_row).start()
            return carry
        lax.fori_loop(0, cnt_s[slot * nc + k], issue, 0)


def _wait_copies(cnt_s, slot, make_copy):
    nc = len(COPY_CHUNKS)
    for k, chunks in enumerate(COPY_CHUNKS):
        def wait_one(j, carry, chunks=chunks):
            make_copy(chunks * CHUNK_ROWS, 0, 0).wait()
            return carry
        lax.fori_loop(0, cnt_s[slot * nc + k], wait_one, 0)


def _block_rows(tab_ref):
    return CHUNK_ROWS * sum(chunks * tab_ref[0, 0, COUNT_BASE + k] for k, chunks in enumerate(COPY_CHUNKS))


def _dispatch_kernel(tab_ref, h2_ref, g_ref, *rest):
    xs_hbm, buf, sem, cnt_s = rest[-4:]
    b = pl.program_id(0)
    nb = pl.num_programs(0)
    slot = b % 2
    ch = CHUNK_ROWS
    tb = TOKEN_BLOCK
    gl = GATE_LANES

    def row_copy(s, rows, local_row, global_row):
        return pltpu.make_async_copy(buf.at[s, pl.ds(local_row, rows), :], xs_hbm.at[pl.ds(global_row, rows), :],
                                     sem.at[s])

    @pl.when(b >= 2)
    def _():
        _wait_copies(cnt_s, slot, functools.partial(row_copy, slot))

    mt = g_ref[...].T > 0.0
    mt_bf = _mask_bf16(mt)
    ti = lax.broadcasted_iota(jnp.int32, (tb, tb), 0)
    tj = lax.broadcasted_iota(jnp.int32, (tb, tb), 1)
    rank = jnp.dot(mt_bf, _mask_bf16(ti < tj), preferred_element_type=F32)
    cnt = jnp.sum(jnp.where(mt, 1.0, 0.0), axis=1, keepdims=True)
    nch = jnp.floor((cnt + (ch - 1.0)) * (1.0 / ch))
    ei = lax.broadcasted_iota(jnp.int32, (gl, gl), 0)
    ej = lax.broadcasted_iota(jnp.int32, (gl, gl), 1)
    loc = ch * jnp.dot(_mask_bf16(ej < ei), jnp.broadcast_to(nch, (gl, gl)).astype(BF16),
                       preferred_element_type=F32)[:, 0:1]
    order = jnp.dot(_mask_bf16(ej <= ei), mt_bf, preferred_element_type=F32)
    dest = loc + rank
    dks = []
    for k in range(TOP_K):
        selk = mt & (order == float(k + 1))
        dk = jnp.sum(jnp.where(selk, dest + 1.0, 0.0), axis=0, keepdims=True) - 1.0
        dks.append(dk.astype(jnp.int32))
    h2 = h2_ref[...]

    def sort_group(rg):
        ri = lax.broadcasted_iota(jnp.int32, (ROW_GROUP, tb), 0) + rg * ROW_GROUP
        hit = (ri == dks[0]) | (ri == dks[1]) | (ri == dks[2]) | (ri == dks[3])
        srt = jnp.dot(_mask_bf16(hit), h2, preferred_element_type=F32)
        buf[slot, rg * ROW_GROUP:(rg + 1) * ROW_GROUP, :] = srt.astype(BF16)

    last = LOCAL_ROWS // ROW_GROUP - 1
    for rg in range(last):
        sort_group(rg)

    @pl.when(_block_rows(tab_ref) > last * ROW_GROUP)
    def _():
        sort_group(last)

    _start_copies(tab_ref, cnt_s, slot, functools.partial(row_copy, slot))

    @pl.when(b == nb - 1)
    def _():
        _wait_copies(cnt_s, slot, functools.partial(row_copy, slot))

        @pl.when(b >= 1)
        def _():
            _wait_copies(cnt_s, 1 - slot, functools.partial(row_copy, 1 - slot))


def _dispatch(table, blk0, h2, gates, rtot, xs_prev=None):
    n, d = h2.shape
    nb = n // TOKEN_BLOCK
    in_specs = [pl.BlockSpec((1, 1, TABLE_W), lambda b: (b + blk0, 0, 0), memory_space=pltpu.SMEM),
                pl.BlockSpec((TOKEN_BLOCK, d), lambda b: (b, 0)),
                pl.BlockSpec((TOKEN_BLOCK, GATE_LANES), lambda b: (b, 0))]
    args = [table, h2, gates]
    aliases = {}
    if xs_prev is not None:
        in_specs.append(pl.BlockSpec(memory_space=pl.ANY))
        args.append(xs_prev)
        aliases = {3: 0}
    return pl.pallas_call(
        _dispatch_kernel,
        grid=(nb,),
        in_specs=in_specs,
        out_specs=pl.BlockSpec(memory_space=pl.ANY),
        out_shape=jax.ShapeDtypeStruct((rtot, d), BF16),
        scratch_shapes=[pltpu.VMEM((2, LOCAL_ROWS, d), BF16),
                        pltpu.SemaphoreType.DMA((2,)),
                        pltpu.SMEM((2 * len(COPY_CHUNKS),), jnp.int32)],
        input_output_aliases=aliases,
        compiler_params=_cparams("arbitrary"),
        name="moe_dispatch",
    )(*args)


def _combine_kernel(tab_ref, tabn_ref, x_ref, g2_ref, g_ref, y_hbm, o_ref, ybuf, sem, cnt_s):
    ch = CHUNK_ROWS
    tb = TOKEN_BLOCK
    gl = GATE_LANES
    b = pl.program_id(0)
    slot = b % 2
    n_rows = _block_rows(tab_ref)

    def row_copy(s, rows, local_row, global_row):
        return pltpu.make_async_copy(y_hbm.at[pl.ds(global_row, rows), :], ybuf.at[s, pl.ds(local_row, rows), :],
                                     sem.at[s])

    @pl.when(b == 0)
    def _():
        _start_copies(tab_ref, cnt_s, slot, functools.partial(row_copy, slot))

    _start_copies(tabn_ref, cnt_s, 1 - slot, functools.partial(row_copy, 1 - slot))

    g = g_ref[...]
    m = g > 0.0
    m_bf = _mask_bf16(m)
    ti = lax.broadcasted_iota(jnp.int32, (tb, tb), 0)
    tj = lax.broadcasted_iota(jnp.int32, (tb, tb), 1)
    rank = jnp.dot(_mask_bf16(tj < ti), m_bf, preferred_element_type=F32)
    cnt = jnp.sum(jnp.where(m, 1.0, 0.0), axis=0, keepdims=True)
    nch = jnp.floor((cnt + (ch - 1.0)) * (1.0 / ch))
    ei = lax.broadcasted_iota(jnp.int32, (gl, gl), 0)
    ej = lax.broadcasted_iota(jnp.int32, (gl, gl), 1)
    loc = ch * jnp.dot(jnp.broadcast_to(nch, (8, gl)).astype(BF16), _mask_bf16(ei < ej),
                       preferred_element_type=F32)[0:1, :]
    order = jnp.dot(m_bf, _mask_bf16(ei <= ej), preferred_element_type=F32)
    dest = loc + rank
    dks, gks = [], []
    for k in range(TOP_K):
        selk = m & (order == float(k + 1))
        dk = jnp.sum(jnp.where(selk, dest + 1.0, 0.0), axis=1, keepdims=True) - 1.0
        dks.append(dk.astype(jnp.int32))
        gks.append(jnp.sum(jnp.where(selk, g, 0.0), axis=1, keepdims=True))

    _wait_copies(cnt_s, slot, functools.partial(row_copy, slot))

    def fold_group(rg):
        li = lax.broadcasted_iota(jnp.int32, (tb, ROW_GROUP), 1) + rg * ROW_GROUP
        cmat = jnp.zeros((tb, ROW_GROUP), F32)
        for dk, gk in zip(dks, gks):
            cmat = cmat + jnp.where(li == dk, gk, 0.0)
        ri = lax.broadcasted_iota(jnp.int32, (ROW_GROUP, 1), 0) + rg * ROW_GROUP
        yv = ybuf[slot, rg * ROW_GROUP:(rg + 1) * ROW_GROUP, :]
        yv = jnp.where(ri < n_rows, yv, jnp.zeros_like(yv))
        return jnp.dot(cmat.astype(BF16), yv, preferred_element_type=F32)

    last = LOCAL_ROWS // ROW_GROUP - 1
    acc = fold_group(0)
    for rg in range(1, last):
        acc = acc + fold_group(rg)
    o_ref[...] = x_ref[...] + g2_ref[0] * acc

    @pl.when(n_rows > last * ROW_GROUP)
    def _():
        o_ref[...] += g2_ref[0] * fold_group(last)


def _combine(table, blk0, x, g2, gates, ys):
    bn, sn, d = x.shape
    n = bn * sn
    per_seq = sn // TOKEN_BLOCK
    n_blocks = n // TOKEN_BLOCK
    empty_row = table.shape[0] - 1
    out = pl.pallas_call(
        _combine_kernel,
        grid=(n_blocks,),
        in_specs=[pl.BlockSpec((1, 1, TABLE_W), lambda b: (b + blk0, 0, 0), memory_space=pltpu.SMEM),
                  pl.BlockSpec((1, 1, TABLE_W),
                               lambda b: (jnp.where(b + 1 < n_blocks, b + blk0 + 1, empty_row), 0, 0),
                               memory_space=pltpu.SMEM),
                  pl.BlockSpec((TOKEN_BLOCK, d), lambda b: (b, 0)),
                  pl.BlockSpec((1, 1, d), lambda b: (b // per_seq, 0, 0)),
                  pl.BlockSpec((TOKEN_BLOCK, GATE_LANES), lambda b: (b, 0)),
                  pl.BlockSpec(memory_space=pl.ANY)],
        out_specs=pl.BlockSpec((TOKEN_BLOCK, d), lambda b: (b, 0)),
        out_shape=jax.ShapeDtypeStruct((n, d), F32),
        scratch_shapes=[pltpu.VMEM((2, LOCAL_ROWS, d), BF16),
                        pltpu.SemaphoreType.DMA((2,)),
                        pltpu.SMEM((2 * len(COPY_CHUNKS),), jnp.int32)],
        compiler_params=_cparams("arbitrary"),
        name="moe_combine",
    )(table, table, x.reshape(n, d), g2, gates.reshape(n, GATE_LANES), ys)
    return out.reshape(bn, sn, d)


TILE_ACTIVE, TILE_FIRST, TILE_SLOT, TILE_COLD = 1, 2, 4, 8
EXPERT_PARTS = 2


def _expert_kernel(te_ref, tf_ref, nv_ref, nx_ref, x_ref, wgu_hbm, bgu_ref, wdn_hbm, bdn_ref, y_ref,
                   wgu_st, wdn_st, wgu_s, wdn_s, sem, *, layer):
    i = pl.program_id(0)
    flag = tf_ref[i]
    d, f2 = wgu_s.shape
    f = f2 // 2
    rows = 128

    def weight_copies(e, s):
        return (pltpu.make_async_copy(wgu_hbm.at[layer, e], wgu_st.at[s], sem.at[0, s]),
                pltpu.make_async_copy(wdn_hbm.at[layer, e], wdn_st.at[s], sem.at[1, s]))

    @pl.when((flag & TILE_FIRST) != 0)
    def _():
        slot = (flag // TILE_SLOT) & 1

        @pl.when((flag & TILE_COLD) != 0)
        def _():
            for cp in weight_copies(te_ref[i], slot):
                cp.start()

        for cp in weight_copies(te_ref[i], slot):
            cp.wait()
        nxt = nx_ref[i]

        @pl.when(nxt >= 0)
        def _():
            for cp in weight_copies(nxt, 1 - slot):
                cp.start()

        def cast_gu(r, c):
            r0 = pl.multiple_of(r * rows, rows)
            wgu_s[pl.ds(r0, rows), :] = wgu_st[slot, pl.ds(r0, rows), :].astype(BF16)
            return c
        lax.fori_loop(0, d // rows, cast_gu, 0)

        def cast_dn(r, c):
            r0 = pl.multiple_of(r * rows, rows)
            wdn_s[pl.ds(r0, rows), :] = wdn_st[slot, pl.ds(r0, rows), :].astype(BF16)
            return c
        lax.fori_loop(0, f // rows, cast_dn, 0)

    @pl.when((flag & TILE_ACTIVE) != 0)
    def _():
        rows = x_ref.shape[0] // EXPERT_PARTS

        def up_project(p):
            ri = lax.broadcasted_iota(jnp.int32, (rows, 1), 0) + p * rows
            x = x_ref[p * rows:(p + 1) * rows, :]
            x = jnp.where(ri < nv_ref[i], x, jnp.zeros_like(x))
            return jnp.dot(x, wgu_s[...], preferred_element_type=F32) + bgu_ref[0]

        gus = [up_project(p) for p in range(EXPERT_PARTS)]
        for p, gu in enumerate(gus):
            gl = jnp.minimum(gu[:, :f], SWIGLU_LIMIT)
            up = jnp.clip(gu[:, f:], -SWIGLU_LIMIT, SWIGLU_LIMIT)
            hid = (up + 1.0) * gl * jax.nn.sigmoid(SWIGLU_ALPHA * gl)
            y = jnp.dot(hid.astype(BF16), wdn_s[...], preferred_element_type=F32) + bdn_ref[0]
            y_ref[p * rows:(p + 1) * rows, :] = y.astype(BF16)

    @pl.when((flag & TILE_ACTIVE) == 0)
    def _():
        y_ref[...] = jnp.zeros_like(y_ref)


def _experts(tile_expert, tile_flag, tile_rows, tile_next, xs, w_gu, b_gu, w_dn, b_dn, layer, tm):
    rtot, d = xs.shape
    _, ne, _, f2 = w_gu.shape
    f = f2 // 2
    grid_spec = pltpu.PrefetchScalarGridSpec(
        num_scalar_prefetch=4,
        grid=(rtot // tm,),
        in_specs=[pl.BlockSpec((tm, d), lambda i, te, tf, nv, nx: (i, 0)),
                  pl.BlockSpec(memory_space=pl.ANY),
                  pl.BlockSpec((1, 1, f2), lambda i, te, tf, nv, nx: (te[i], 0, 0)),
                  pl.BlockSpec(memory_space=pl.ANY),
                  pl.BlockSpec((1, 1, d), lambda i, te, tf, nv, nx: (te[i], 0, 0))],
        out_specs=pl.BlockSpec((tm, d), lambda i, te, tf, nv, nx: (i, 0)),
        scratch_shapes=[pltpu.VMEM((2, d, f2), F32), pltpu.VMEM((2, f, d), F32),
                        pltpu.VMEM((d, f2), BF16), pltpu.VMEM((f, d), BF16),
                        pltpu.SemaphoreType.DMA((2, 2))],
    )
    return pl.pallas_call(
        functools.partial(_expert_kernel, layer=layer),
        grid_spec=grid_spec,
        out_shape=jax.ShapeDtypeStruct((rtot, d), BF16),
        compiler_params=_cparams("arbitrary"),
        name="experts",
    )(tile_expert, tile_flag, tile_rows, tile_next, xs, w_gu, b_gu.reshape(ne, 1, f2), w_dn,
      b_dn.reshape(ne, 1, d))


def _moe_layout(gate_streams, tm):
    n = sum(gs.shape[0] for gs in gate_streams)
    nb = n // TOKEN_BLOCK
    ne = N_EXPERTS
    ch = CHUNK_ROWS
    tmc = tm // ch
    n_tiles = -(-(n * TOP_K + nb * ne * (ch - 1) + ne * (tm - ch)) // tm)
    i32 = jnp.int32
    cnt = jnp.concatenate(
        [jnp.sum((gs[:, :ne] > 0.0).astype(i32).reshape(-1, TOKEN_BLOCK, ne), axis=1) for gs in gate_streams], axis=0)
    nch = (cnt + ch - 1) // ch
    loc_end = jnp.cumsum(nch, axis=1)
    loc_off = loc_end - nch
    exp_ch = jnp.sum(nch, axis=0)
    exp_pad = ((exp_ch + tmc - 1) // tmc) * tmc
    exp_end = jnp.cumsum(exp_pad)
    exp_off = exp_end - exp_pad
    seg_off = exp_off[None, :] + jnp.cumsum(nch, axis=0) - nch
    per_seg = (nch // 4, (nch % 4) // 2, nch % 2)
    seg_start = (0 * nch, 4 * per_seg[0], 4 * per_seg[0] + 2 * per_seg[1])
    sections, totals = [], []
    for cnt_k, start_k, chunks, slots in zip(per_seg, seg_start, COPY_CHUNKS, COPY_SLOTS):
        cum = jnp.cumsum(cnt_k, axis=1)
        j = jnp.arange(slots, dtype=i32)
        e_of_j = jnp.minimum(jnp.sum((cum[:, None, :] <= j[None, :, None]).astype(i32), axis=2), ne - 1)
        hot = (e_of_j[:, :, None] == jnp.arange(ne, dtype=i32)).astype(i32)
        within = chunks * (j[None, :] - jnp.sum(hot * (cum - cnt_k)[:, None, :], axis=2))
        local = jnp.sum(hot * (loc_off + start_k)[:, None, :], axis=2) + within
        glob = jnp.sum(hot * (seg_off + start_k)[:, None, :], axis=2) + within
        sections.append(jnp.where(j[None, :] < cum[:, ne - 1:], glob * 256 + local, 0))
        totals.append(cum[:, ne - 1:])
    pad = jnp.zeros((nb, COUNT_BASE - sum(COPY_SLOTS)), i32)
    table = jnp.concatenate(sections + [pad] + totals, axis=1)
    table = jnp.concatenate([table, jnp.zeros((1, TABLE_W), i32)], axis=0)
    tile_start = jnp.arange(n_tiles, dtype=i32) * tmc
    te = jnp.minimum(jnp.sum((exp_end[None, :] <= tile_start[:, None]).astype(i32), axis=1), ne - 1)
    te_hot = (te[:, None] == jnp.arange(ne, dtype=i32)).astype(i32)
    t_off = jnp.sum(te_hot * exp_off[None, :], axis=1)
    t_len = jnp.sum(te_hot * exp_ch[None, :], axis=1)
    total = exp_end[ne - 1]
    active = tile_start < total
    first = active & (tile_start == t_off)
    tile_rows = jnp.clip((t_len - (tile_start - t_off)) * ch, 0, tm)
    last_active = (tile_start + tmc == total).astype(i32)
    te = jnp.where(active, te, jnp.sum(last_active * te))
    group = jnp.cumsum(first.astype(i32)) - 1
    ei = jnp.arange(ne, dtype=i32)
    later = (ei[None, :] > ei[:, None]) & (exp_ch[None, :] > 0)
    next_e = jnp.min(jnp.where(later, ei[None, :], ne), axis=1)
    next_e = jnp.where(next_e < ne, next_e, -1)
    tile_next = jnp.sum(te_hot * next_e[None, :], axis=1)
    tile_flag = (TILE_ACTIVE * active.astype(i32) + TILE_FIRST * first.astype(i32) + TILE_SLOT * (group & 1)
                 + TILE_COLD * (first & (group == 0)).astype(i32))
    return table.reshape(nb + 1, 1, TABLE_W), (te, tile_flag, tile_rows, tile_next), n_tiles * tm


def _rope_tables(sn, scale_q, roped):
    if not roped:
        ones = jnp.ones((sn, GROUP_W), F32)
        zeros = jnp.zeros((sn, GROUP_W), F32)
        return ones * scale_q, zeros, ones, zeros
    pos = jnp.arange(sn)
    row = (pos // GRID_W).astype(F32)
    col = (pos % GRID_W).astype(F32)
    nf = HEAD_QK_B // 4
    inv = ROPE_BASE ** (-jnp.arange(nf, dtype=F32) / nf)
    ar = row[:, None] * inv
    ac = col[:, None] * inv
    cos32 = jnp.concatenate([jnp.cos(ar), jnp.cos(ar), jnp.cos(ac), jnp.cos(ac)], axis=1)
    sin32 = jnp.concatenate([-jnp.sin(ar), jnp.sin(ar), -jnp.sin(ac), jnp.sin(ac)], axis=1)
    reps = GROUP_W // HEAD_QK_B
    cos = jnp.tile(cos32, (1, reps))
    sin = jnp.tile(sin32, (1, reps))
    return cos * scale_q, sin * scale_q, cos, sin


def kernel(x, c, ctx, c_ctx, w_mod, b_mod, norm1_g, norm2_g, w_in, a_vnorm_g, a_ws, a_bs, b_qnorm_g, b_knorm_g, b_lam_q1, b_lam_k1, b_lam_q2, b_lam_k2, b_subln_g, c_conv_w, c_conv_b, c_ln_g, c_ln_b, d_conv_w, w_out, router_w, router_b, exp_w_gu, exp_b_gu, exp_w_dn, exp_b_dn):
    bsz, seq, dm = x.shape
    ctx_len = ctx.shape[1]
    depth = w_in.shape[0]
    g = GROUP_W
    tm_lat = min(512, seq)
    tm_ctx = min(512, ctx_len)
    tq_lat = min(512, seq)
    tq_ctx = min(256, ctx_len)
    tm_moe = 512
    assert seq % TOKEN_BLOCK == 0 and ctx_len % TOKEN_BLOCK == 0

    n_mod_rows = -(-(bsz + 1) // 8) * 8
    cc = jnp.concatenate([c, c_ctx[None, :], jnp.zeros((n_mod_rows - bsz - 1, dm), F32)], axis=0)
    mods = _modulation(cc, w_mod, b_mod)

    gsum32 = _group_ones(g, HEAD_QK_B)
    gsum64 = _group_ones(g, HEAD_V_B)
    scale_q = HEAD_QK_B ** -0.5 * math.log2(math.e)
    tabs_lat = _rope_tables(seq, scale_q, True)
    tabs_ctx = _rope_tables(ctx_len, scale_q, False)

    xc = ctx
    for l in range(depth):
        last = l == depth - 1
        lam_init = 0.8 - 0.6 * math.exp(-0.3 * l)
        lam = (jnp.exp(jnp.sum(b_lam_q1[l] * b_lam_k1[l])) - jnp.exp(jnp.sum(b_lam_q2[l] * b_lam_k2[l]))
               + lam_init).astype(F32)
        score_bound = 1.01 * scale_q * HEAD_QK_B * jnp.max(jnp.abs(b_qnorm_g[l])) * jnp.max(jnp.abs(b_knorm_g[l]))
        lam = jnp.stack([lam, score_bound.astype(F32)])
        m_lat = mods[l, :bsz].reshape(bsz, 6, 1, dm)
        m_ctx = jnp.broadcast_to(mods[l, bsz].reshape(1, 6, 1, dm), (bsz, 6, 1, dm))
        sh1, sc1, g1, sh2, sc2, g2 = [m_lat[:, i] for i in range(6)]
        sh1c, sc1c, g1c, sh2c, sc2c, g2c = [m_ctx[:, i] for i in range(6)]

        w_in_bf = w_in[l].astype(BF16)
        n1g = norm1_g[l][None, :]
        n2g = norm2_g[l][None, :]
        reps = g // HEAD_QK_B // 2
        qg = jnp.tile(b_qnorm_g[l].reshape(1, 2 * HEAD_QK_B), (1, reps))
        kg = jnp.tile(b_knorm_g[l].reshape(1, 2 * HEAD_QK_B), (1, reps))
        gpost = jnp.tile(b_subln_g[l][None, :], (1, N_HEADS_B)) * (1.0 - lam_init)
        vg = a_vnorm_g[l][None, :]
        ws_bf = a_ws[l].astype(BF16)
        bs_exp = jnp.repeat(a_bs[l].T, HEAD_A, axis=1)
        mix_params = (vg, ws_bf, bs_exp, gsum64, c_conv_w[l], c_conv_b[l][None, :], c_ln_g[l][None, :],
                      c_ln_b[l][None, :], d_conv_w[l])
        wo_bf = w_out[l].astype(BF16)
        wr = jnp.pad(router_w[l], ((0, 0), (0, GATE_LANES - N_EXPERTS)))
        wr_hi = wr.astype(BF16)
        wr_lo = (wr - wr_hi.astype(F32)).astype(BF16)
        br = jnp.pad(router_b[l], (0, GATE_LANES - N_EXPERTS))[None, :]

        p_lat = _in_proj(x, sc1, sh1, n1g, w_in_bf, gsum32, qg, kg, tabs_lat, tm_lat)
        p_ctx = _in_proj(xc, sc1c, sh1c, n1g, w_in_bf, gsum32, qg, kg, tabs_ctx, tm_ctx)

        yb = _attention(lam, p_lat, (p_lat, p_ctx), gpost, gsum64, tq_lat)
        ya, yc, yd = _mixers(p_lat, *mix_params)
        x, h2, gates = _out_proj(ya, yb, yc, yd, x, g1, sc2, sh2, n2g, wo_bf, wr_hi, wr_lo, br, tm_lat)
        moe_w = (exp_w_gu, exp_b_gu[l], exp_w_dn, exp_b_dn[l], l, tm_moe)
        h2f = h2.reshape(-1, dm)
        gatesf = gates.reshape(-1, GATE_LANES)
        if last:
            table, tiles, rtot = _moe_layout((gatesf,), tm_moe)
            xs = _dispatch(table, 0, h2f, gatesf, rtot)
            ys = _experts(*tiles, xs, *moe_w)
            x = _combine(table, 0, x, g2, gates, ys)
        else:
            ybc = _attention(lam, p_ctx, (p_ctx,), gpost, gsum64, tq_ctx)
            yac, ycc, ydc = _mixers(p_ctx, *mix_params)
            xc, h2c, gatesc = _out_proj(yac, ybc, ycc, ydc, xc, g1c, sc2c, sh2c, n2g, wo_bf, wr_hi, wr_lo, br,
                                        tm_ctx)
            gatescf = gatesc.reshape(-1, GATE_LANES)
            blk_ctx = bsz * seq // TOKEN_BLOCK
            table, tiles, rtot = _moe_layout((gatesf, gatescf), tm_moe)
            xs = _dispatch(table, 0, h2f, gatesf, rtot)
            xs = _dispatch(table, blk_ctx, h2c.reshape(-1, dm), gatescf, rtot, xs_prev=xs)
            ys = _experts(*tiles, xs, *moe_w)
            x = _combine(table, 0, x, g2, gates, ys)
            xc = _combine(table, blk_ctx, xc, g2c, gatesc, ys)
    return x
```

```python
import functools
import math

import jax
import jax.numpy as jnp
from jax import lax
from jax.experimental import pallas as pl
from jax.experimental.pallas import tpu as pltpu

F32 = jnp.float32
BF16 = jnp.bfloat16

GROUP_W = 256
N_SPLITS = 10
N_HEADS_A = 4
HEAD_A = 64
CHUNK = 128
N_HEADS_B = 4
HEAD_V_B = 64
HEAD_QK_B = 32
GRID_W = 64
ROPE_BASE = 10000.0
CONV_C = 31
CONV_D = 3
N_EXPERTS = 32
TOP_K = 4
SWIGLU_LIMIT = 7.0
SWIGLU_ALPHA = 1.702
EPS = 1e-6

GATE_LANES = 128
CONV_HALO = 16
CONV_SPAN = CHUNK + 8 * ((CONV_HALO - CONV_C // 2 + CONV_C - 1) // 8)
VMEM_LIMIT = 56 * 1024 * 1024
MAX_FIXED_SHIFT = 60.0
ATTN_LOOKAHEAD = 2

TOKEN_BLOCK = 256
ROW_DTYPE = F32
CHUNK_ROWS = 8
ROW_GROUP = 256
LOCAL_ROWS = -(-(TOKEN_BLOCK * TOP_K + N_EXPERTS * (CHUNK_ROWS - 1)) // ROW_GROUP) * ROW_GROUP
MAX_CHUNKS = LOCAL_ROWS // CHUNK_ROWS
COPY_CHUNKS = (4, 2, 1)
COPY_SLOTS = (MAX_CHUNKS // 4, N_EXPERTS, N_EXPERTS)
COPY_BASE = (0, COPY_SLOTS[0], COPY_SLOTS[0] + COPY_SLOTS[1])
TABLE_W = 128
COUNT_BASE = TABLE_W - len(COPY_CHUNKS)
assert MAX_CHUNKS <= 256 and sum(COPY_SLOTS) <= COUNT_BASE


def _cparams(*sem):
    return pltpu.CompilerParams(dimension_semantics=sem, vmem_limit_bytes=VMEM_LIMIT)


def _group_ones(width, group):
    i = jnp.arange(width) // group
    return (i[:, None] == i[None, :]).astype(BF16)


def _mod_kernel(c_ref, w_ref, b_ref, o_ref):
    c = c_ref[...]
    a = (c * jax.nn.sigmoid(c)).astype(BF16)
    o_ref[0] = jnp.dot(a, w_ref[0].astype(BF16), preferred_element_type=F32) + b_ref[0]


def _modulation(cc, w_mod, b_mod):
    nl, d, d6 = w_mod.shape
    r = cc.shape[0]
    return pl.pallas_call(
        _mod_kernel,
        grid=(nl, d6 // d),
        in_specs=[pl.BlockSpec((r, d), lambda l, j: (0, 0)),
                  pl.BlockSpec((1, d, d), lambda l, j: (l, 0, j)),
                  pl.BlockSpec((1, 1, d), lambda l, j: (l, 0, j))],
        out_specs=pl.BlockSpec((1, r, d), lambda l, j: (l, 0, j)),
        out_shape=jax.ShapeDtypeStruct((nl, r, d6), F32),
        compiler_params=_cparams("arbitrary", "arbitrary"),
        name="modulation",
    )(cc, w_mod, b_mod.reshape(nl, 1, d6))


def _rope_norm(t, gsum, gain, cos, sin, lo_lane):
    ss = jnp.dot((t * t).astype(BF16), gsum, preferred_element_type=F32)
    tn = t * lax.rsqrt(ss * (1.0 / HEAD_QK_B) + EPS) * gain
    rot = jnp.where(lo_lane, pltpu.roll(tn, GROUP_W - 8, 1), pltpu.roll(tn, 8, 1))
    return tn * cos + rot * sin


def _in_kernel(x_ref, sc_ref, sh_ref, g_ref, w_ref, gsum_ref, qg_ref, kg_ref,
               cq_ref, sq_ref, ck_ref, sk_ref, o_ref):
    g = GROUP_W
    tm = x_ref.shape[1]
    n_parts = 2 if tm % 16 == 0 else 1
    rows = tm // n_parts
    lane = lax.broadcasted_iota(jnp.int32, (1, g), 1)
    lo_lane = (lane & 8) == 0
    gsum = gsum_ref[...]

    def project(r0):
        x = x_ref[0, r0:r0 + rows, :]
        ms = jnp.mean(x * x, axis=-1, keepdims=True)
        h = (x * lax.rsqrt(ms + EPS) * g_ref[...]) * (1.0 + sc_ref[0]) + sh_ref[0]
        return jnp.dot(h.astype(BF16), w_ref[...], preferred_element_type=F32)

    def finish(r0, p):
        sl = slice(r0, r0 + rows)
        q = _rope_norm(p[:, 2 * g:3 * g], gsum, qg_ref[...], cq_ref[sl, :], sq_ref[sl, :], lo_lane)
        k = _rope_norm(p[:, 3 * g:4 * g], gsum, kg_ref[...], ck_ref[sl, :], sk_ref[sl, :], lo_lane)
        o_ref[0, sl, 0:2 * g] = p[:, 0:2 * g].astype(BF16)
        o_ref[0, sl, 2 * g:3 * g] = q.astype(BF16)
        o_ref[0, sl, 3 * g:4 * g] = k.astype(BF16)
        o_ref[0, sl, 4 * g:] = p[:, 4 * g:].astype(BF16)

    ps = [project(i * rows) for i in range(n_parts)]
    for i, p in enumerate(ps):
        finish(i * rows, p)


def _in_proj(x, sc, sh, g, w_bf, gsum32, qg, kg, tabs, tm):
    bn, sn, d = x.shape
    d_in = w_bf.shape[1]
    tab_spec = pl.BlockSpec((tm, GROUP_W), lambda b, s: (s, 0))
    vec_spec = pl.BlockSpec((1, 1, d), lambda b, s: (b, 0, 0))
    row256 = pl.BlockSpec((1, GROUP_W), lambda b, s: (0, 0))
    return pl.pallas_call(
        _in_kernel,
        grid=(bn, sn // tm),
        in_specs=[pl.BlockSpec((1, tm, d), lambda b, s: (b, s, 0)),
                  vec_spec, vec_spec,
                  pl.BlockSpec((1, d), lambda b, s: (0, 0)),
                  pl.BlockSpec((d, d_in), lambda b, s: (0, 0)),
                  pl.BlockSpec((GROUP_W, GROUP_W), lambda b, s: (0, 0)),
                  row256, row256, tab_spec, tab_spec, tab_spec, tab_spec],
        out_specs=pl.BlockSpec((1, tm, d_in), lambda b, s: (b, s, 0)),
        out_shape=jax.ShapeDtypeStruct((bn, sn, d_in), BF16),
        compiler_params=_cparams("parallel", "arbitrary"),
        name="in_proj",
    )(x, sc, sh, g, w_bf, gsum32, qg, kg, *tabs)


def _mix_kernel(au_ref, av_ref, ca_ref, cg_ref, db_ref, dc_ref, dh_ref,
                vg_ref, ws_ref, bs_ref, gsum_ref, cw_ref, cb_ref, lng_ref, lnb_ref, dw_ref,
                ya_ref, yc_ref, yd_ref, zc_ref, zd_ref, zs_ref, *, sn):
    g = GROUP_W
    lane = lax.broadcasted_iota(jnp.int32, (1, g), 1)
    n_chunks = sn // CHUNK

    def gmlp_chunk(ci, carry):
        r0 = pl.multiple_of(ci * CHUNK, CHUNK)
        u = jax.nn.gelu(au_ref[0, pl.ds(r0, CHUNK), :].astype(F32))
        v = jax.nn.gelu(av_ref[0, pl.ds(r0, CHUNK), :].astype(F32))
        ss = jnp.dot((v * v).astype(BF16), gsum_ref[...], preferred_element_type=F32)
        vn = (v * lax.rsqrt(ss * (1.0 / HEAD_A) + EPS) * vg_ref[...]).astype(BF16)
        mixed = bs_ref[...]
        for h in range(N_HEADS_A):
            mh = jnp.dot(ws_ref[h], vn, preferred_element_type=F32)
            hm = (lane >= h * HEAD_A) & (lane < (h + 1) * HEAD_A)
            mixed = mixed + jnp.where(hm, mh, 0.0)
        ya_ref[0, pl.ds(r0, CHUNK), :] = (u * mixed).astype(BF16)
        return carry

    lax.fori_loop(0, n_chunks, gmlp_chunk, 0, unroll=2)

    zeros_halo = jnp.zeros((CONV_HALO, g), F32)
    zc_ref[0:CONV_HALO, :] = zeros_halo
    zc_ref[CONV_HALO + sn:CONV_HALO + sn + CONV_HALO, :] = zeros_halo
    zd_ref[0:CONV_HALO, :] = zeros_halo
    zd_ref[CONV_HALO + sn:CONV_HALO + sn + CONV_HALO, :] = zeros_halo

    def stage_chunk(ci, carry):
        r0 = pl.multiple_of(ci * CHUNK, CHUNK)
        ca = ca_ref[0, pl.ds(r0, CHUNK), :].astype(F32)
        cg = cg_ref[0, pl.ds(r0, CHUNK), :].astype(F32)
        zc_ref[pl.ds(r0 + CONV_HALO, CHUNK), :] = ca * jax.nn.sigmoid(cg)
        dc = dc_ref[0, pl.ds(r0, CHUNK), :].astype(F32)
        dh = dh_ref[0, pl.ds(r0, CHUNK), :].astype(F32)
        zd_ref[pl.ds(r0 + CONV_HALO, CHUNK), :] = dc * dh
        return carry

    lax.fori_loop(0, n_chunks, stage_chunk, 0)

    def conv_chunk(ci, carry):
        r0 = pl.multiple_of(ci * CHUNK, CHUNK)
        win = zc_ref[pl.ds(r0, CHUNK + 2 * CONV_HALO), :]
        acc = jnp.zeros((CHUNK, g), F32) + cb_ref[...]
        base = CONV_HALO - CONV_C // 2
        for ph in range(8):
            taps = [k for k in range(CONV_C) if (base + k) % 8 == ph]
            if taps:
                zs_ref[ph] = win[ph:ph + CONV_SPAN, :]
                for k in taps:
                    a = (base + k) // 8 * 8
                    acc = acc + cw_ref[k:k + 1, :] * zs_ref[ph, a:a + CHUNK, :]
        mu = jnp.mean(acc, axis=-1, keepdims=True)
        dev = acc - mu
        var = jnp.mean(dev * dev, axis=-1, keepdims=True)
        yln = dev * lax.rsqrt(var + EPS) * lng_ref[...] + lnb_ref[...]
        yc_ref[0, pl.ds(r0, CHUNK), :] = (yln * jax.nn.sigmoid(yln)).astype(BF16)

        wind = zd_ref[pl.ds(r0, CHUNK + 2 * CONV_HALO), :]
        accd = jnp.zeros((CHUNK, g), F32)
        based = CONV_HALO - CONV_D // 2
        for k in range(CONV_D):
            accd = accd + dw_ref[k:k + 1, :] * wind[based + k:based + k + CHUNK, :]
        db = db_ref[0, pl.ds(r0, CHUNK), :].astype(F32)
        yd_ref[0, pl.ds(r0, CHUNK), :] = (db * accd).astype(BF16)
        return carry

    lax.fori_loop(0, n_chunks, conv_chunk, 0)


def _mixers(p, vg, ws_bf, bs_exp, gsum64, cw, cb, lng, lnb, dw):
    bn, sn, _ = p.shape
    g = GROUP_W

    def col(j):
        return pl.BlockSpec((1, sn, g), lambda b, j=j: (b, 0, j))

    def full(a):
        nd = a.ndim
        return pl.BlockSpec(a.shape, lambda b, nd=nd: (0,) * nd)

    out_spec = pl.BlockSpec((1, sn, g), lambda b: (b, 0, 0))
    out_sds = jax.ShapeDtypeStruct((bn, sn, g), BF16)
    params = (vg, ws_bf, bs_exp, gsum64, cw, cb, lng, lnb, dw)
    return pl.pallas_call(
        functools.partial(_mix_kernel, sn=sn),
        grid=(bn,),
        in_specs=[col(0), col(1), col(5), col(6), col(7), col(8), col(9)] + [full(a) for a in params],
        out_specs=[out_spec, out_spec, out_spec],
        out_shape=[out_sds, out_sds, out_sds],
        scratch_shapes=[pltpu.VMEM((sn + 2 * CONV_HALO, g), F32),
                        pltpu.VMEM((sn + 2 * CONV_HALO, g), F32),
                        pltpu.VMEM((8, CONV_SPAN, g), F32)],
        compiler_params=_cparams("parallel"),
        name="mixers",
    )(p, p, p, p, p, p, p, *params)


def _attn_kernel(lam_ref, q_ref, *rest, n_seg):
    k_refs = rest[0:2 * n_seg:2]
    v_refs = rest[1:2 * n_seg:2]
    gpost_ref, gsum_ref, o_ref = rest[2 * n_seg:]
    lam = lam_ref[0]
    score_bound = lam_ref[1]
    q = q_ref[0]
    ks = [r[0] for r in k_refs]
    vs = [r[0] for r in v_refs]
    tq = q.shape[0]
    lane = lax.broadcasted_iota(jnp.int32, (1, GROUP_W), 1)

    def scores(u):
        lo = u * HEAD_QK_B
        qm = jnp.where((lane >= lo) & (lane < lo + HEAD_QK_B), q, jnp.zeros_like(q))
        return [lax.dot_general(qm, k, (((1,), (1,)), ((), ())), preferred_element_type=F32) for k in ks]

    def softmax_parts(ss, bounded):
        if bounded:
            mx = score_bound
        else:
            mx = jnp.max(ss[0], axis=-1, keepdims=True)
            for s in ss[1:]:
                mx = jnp.maximum(mx, jnp.max(s, axis=-1, keepdims=True))
        es = [jnp.exp2(s - mx) for s in ss]
        den = jnp.sum(es[0], axis=-1, keepdims=True)
        for e in es[1:]:
            den = den + jnp.sum(e, axis=-1, keepdims=True)
        return [e.astype(BF16) for e in es], den

    def attend(bounded):
        y = jnp.zeros((tq, GROUP_W), F32)
        n_units = 2 * N_HEADS_B
        pending = {u: scores(u) for u in range(min(ATTN_LOOKAHEAD, n_units))}
        parts = {}
        for u in range(n_units):
            if u + ATTN_LOOKAHEAD < n_units:
                pending[u + ATTN_LOOKAHEAD] = scores(u + ATTN_LOOKAHEAD)
            parts[u] = softmax_parts(pending.pop(u), bounded)
            if u % 2 == 1:
                h = u // 2
                (num1, den1), (num2, den2) = parts.pop(u - 1), parts.pop(u)
                coef = (lam * den1 / den2).astype(BF16)
                o = jnp.zeros((tq, GROUP_W), F32)
                for e1, e2, v in zip(num1, num2, vs):
                    o = o + jnp.dot(e1 - coef * e2, v, preferred_element_type=F32)
                o = o * (1.0 / den1)
                y = jnp.where((lane >= h * HEAD_V_B) & (lane < (h + 1) * HEAD_V_B), o, y)
        ss = jnp.dot((y * y).astype(BF16), gsum_ref[...], preferred_element_type=F32)
        o_ref[0] = (y * lax.rsqrt(ss * (1.0 / HEAD_V_B) + EPS) * gpost_ref[...]).astype(BF16)

    small = score_bound <= MAX_FIXED_SHIFT

    @pl.when(small)
    def _():
        attend(True)

    @pl.when(jnp.logical_not(small))
    def _():
        attend(False)


def _attention(lam, p_q, kv_sources, gpost, gsum64, tq):
    bn, sq, _ = p_q.shape
    g = GROUP_W
    kv_specs, kv_args = [], []
    for p in kv_sources:
        sk = p.shape[1]
        kv_specs += [pl.BlockSpec((1, sk, g), lambda b, i: (b, 0, 3)),
                     pl.BlockSpec((1, sk, g), lambda b, i: (b, 0, 4))]
        kv_args += [p, p]
    return pl.pallas_call(
        functools.partial(_attn_kernel, n_seg=len(kv_sources)),
        grid=(bn, sq // tq),
        in_specs=[pl.BlockSpec(memory_space=pltpu.SMEM),
                  pl.BlockSpec((1, tq, g), lambda b, i: (b, i, 2))] + kv_specs
                 + [pl.BlockSpec((1, g), lambda b, i: (0, 0)),
                    pl.BlockSpec((g, g), lambda b, i: (0, 0))],
        out_specs=pl.BlockSpec((1, tq, g), lambda b, i: (b, i, 0)),
        out_shape=jax.ShapeDtypeStruct((bn, sq, g), BF16),
        compiler_params=_cparams("parallel", "arbitrary"),
        name="diff_attention",
    )(lam, p_q, *kv_args, gpost, gsum64)


def _out_kernel(ya_ref, yb_ref, yc_ref, yd_ref, x_ref, g1_ref, sc2_ref, sh2_ref, n2g_ref,
                wo_ref, wrh_ref, wrl_ref, br_ref, xo_ref, h2_ref, gate_ref):
    g = GROUP_W
    tm = x_ref.shape[1]
    n_parts = 2 if tm % 16 == 0 else 1
    rows = tm // n_parts

    def project(r0):
        sl = slice(r0, r0 + rows)
        acc = jnp.dot(ya_ref[0, sl, :], wo_ref[0:g, :], preferred_element_type=F32)
        acc = acc + jnp.dot(yb_ref[0, sl, :], wo_ref[g:2 * g, :], preferred_element_type=F32)
        acc = acc + jnp.dot(yc_ref[0, sl, :], wo_ref[2 * g:3 * g, :], preferred_element_type=F32)
        return acc + jnp.dot(yd_ref[0, sl, :], wo_ref[3 * g:4 * g, :], preferred_element_type=F32)

    def finish(r0, acc):
        sl = slice(r0, r0 + rows)
        xn = x_ref[0, sl, :] + g1_ref[0] * acc
        xo_ref[0, sl, :] = xn
        ms = jnp.mean(xn * xn, axis=-1, keepdims=True)
        h2 = (xn * lax.rsqrt(ms + EPS) * n2g_ref[...]) * (1.0 + sc2_ref[0]) + sh2_ref[0]
        hh = h2.astype(BF16)
        h2_ref[0, sl, :] = hh
        hl = (h2 - hh.astype(F32)).astype(BF16)
        logits = (jnp.dot(hh, wrh_ref[...], preferred_element_type=F32)
                  + jnp.dot(hl, wrh_ref[...], preferred_element_type=F32)
                  + jnp.dot(hh, wrl_ref[...], preferred_element_type=F32)) + br_ref[...]
        lane = lax.broadcasted_iota(jnp.int32, (rows, GATE_LANES), 1)
        neg_inf = jnp.float32(-jnp.inf)
        l = jnp.where(lane < N_EXPERTS, logits, neg_inf)
        vals, sels = [], []
        for _ in range(TOP_K):
            m = jnp.max(l, axis=-1, keepdims=True)
            idx = jnp.min(jnp.where(l == m, lane, GATE_LANES), axis=-1, keepdims=True)
            sel = lane == idx
            vals.append(m)
            sels.append(sel)
            l = jnp.where(sel, neg_inf, l)
        es = [jnp.exp(vk - vals[0]) for vk in vals]
        inv = 1.0 / (es[0] + es[1] + es[2] + es[3])
        gates = jnp.zeros((rows, GATE_LANES), F32)
        for sel, ek in zip(sels, es):
            gates = jnp.where(sel, ek * inv, gates)
        gate_ref[0, sl, :] = gates

    accs = [project(p * rows) for p in range(n_parts)]
    for p, acc in enumerate(accs):
        finish(p * rows, acc)


def _out_proj(ya, yb, yc, yd, x, g1, sc2, sh2, n2g, wo_bf, wr_hi, wr_lo, br, tm):
    bn, sn, d = x.shape
    g = GROUP_W
    yspec = pl.BlockSpec((1, tm, g), lambda b, s: (b, s, 0))
    vec_spec = pl.BlockSpec((1, 1, d), lambda b, s: (b, 0, 0))
    xspec = pl.BlockSpec((1, tm, d), lambda b, s: (b, s, 0))
    return pl.pallas_call(
        _out_kernel,
        grid=(bn, sn // tm),
        in_specs=[yspec, yspec, yspec, yspec, xspec, vec_spec, vec_spec, vec_spec,
                  pl.BlockSpec((1, d), lambda b, s: (0, 0)),
                  pl.BlockSpec((d, d), lambda b, s: (0, 0)),
                  pl.BlockSpec((d, GATE_LANES), lambda b, s: (0, 0)),
                  pl.BlockSpec((d, GATE_LANES), lambda b, s: (0, 0)),
                  pl.BlockSpec((1, GATE_LANES), lambda b, s: (0, 0))],
        out_specs=[xspec, xspec, pl.BlockSpec((1, tm, GATE_LANES), lambda b, s: (b, s, 0))],
        out_shape=[jax.ShapeDtypeStruct((bn, sn, d), F32),
                   jax.ShapeDtypeStruct((bn, sn, d), BF16),
                   jax.ShapeDtypeStruct((bn, sn, GATE_LANES), F32)],
        compiler_params=_cparams("parallel", "arbitrary"),
        name="out_proj",
    )(ya, yb, yc, yd, x, g1, sc2, sh2, n2g, wo_bf, wr_hi, wr_lo, br)


def _mask_bf16(m):
    return jnp.where(m, 1.0, 0.0).astype(BF16)


def _start_copies(tab_ref, cnt_s, slot, make_copy):
    nc = len(COPY_CHUNKS)
    for k, chunks in enumerate(COPY_CHUNKS):
        cnt_s[slot * nc + k] = tab_ref[0, 0, COUNT_BASE + k]

        def issue(j, carry, k=k, chunks=chunks):
            v = tab_ref[0, 0, COPY_BASE[k] + j]
            local_row = pl.multiple_of((v & 255) * CHUNK_ROWS, CHUNK_ROWS)
            global_row = pl.multiple_of((v >> 8) * CHUNK_ROWS, CHUNK_ROWS)
            make_copy(chunks * CHUNK_ROWS, local_row, global_row).start()
            return carry
        lax.fori_loop(0, cnt_s[slot * nc + k], issue, 0)


def _wait_copies(cnt_s, slot, make_copy):
    nc = len(COPY_CHUNKS)
    for k, chunks in enumerate(COPY_CHUNKS):
        def wait_one(j, carry, chunks=chunks):
            make_copy(chunks * CHUNK_ROWS, 0, 0).wait()
            return carry
        lax.fori_loop(0, cnt_s[slot * nc + k], wait_one, 0)


def _block_rows(tab_ref):
    return CHUNK_ROWS * sum(chunks * tab_ref[0, 0, COUNT_BASE + k] for k, chunks in enumerate(COPY_CHUNKS))


def _dispatch_kernel(tab_ref, h2_ref, g_ref, *rest):
    xs_hbm, buf, sem, cnt_s = rest[-4:]
    b = pl.program_id(0)
    nb = pl.num_programs(0)
    slot = b % 2
    ch = CHUNK_ROWS
    tb = TOKEN_BLOCK
    gl = GATE_LANES

    def row_copy(s, rows, local_row, global_row):
        return pltpu.make_async_copy(buf.at[s, pl.ds(local_row, rows), :], xs_hbm.at[pl.ds(global_row, rows), :],
                                     sem.at[s])

    @pl.when(b >= 2)
    def _():
        _wait_copies(cnt_s, slot, functools.partial(row_copy, slot))

    mt = g_ref[...].T > 0.0
    mt_bf = _mask_bf16(mt)
    ti = lax.broadcasted_iota(jnp.int32, (tb, tb), 0)
    tj = lax.broadcasted_iota(jnp.int32, (tb, tb), 1)
    rank = jnp.dot(mt_bf, _mask_bf16(ti < tj), preferred_element_type=F32)
    cnt = jnp.sum(jnp.where(mt, 1.0, 0.0), axis=1, keepdims=True)
    nch = jnp.floor((cnt + (ch - 1.0)) * (1.0 / ch))
    ei = lax.broadcasted_iota(jnp.int32, (gl, gl), 0)
    ej = lax.broadcasted_iota(jnp.int32, (gl, gl), 1)
    loc = ch * jnp.dot(_mask_bf16(ej < ei), jnp.broadcast_to(nch, (gl, gl)).astype(BF16),
                       preferred_element_type=F32)[:, 0:1]
    order = jnp.dot(_mask_bf16(ej <= ei), mt_bf, preferred_element_type=F32)
    dest = loc + rank
    dks = []
    for k in range(TOP_K):
        selk = mt & (order == float(k + 1))
        dk = jnp.sum(jnp.where(selk, dest + 1.0, 0.0), axis=0, keepdims=True) - 1.0
        dks.append(dk.astype(jnp.int32))
    h2 = h2_ref[...]

    def sort_group(rg):
        ri = lax.broadcasted_iota(jnp.int32, (ROW_GROUP, tb), 0) + rg * ROW_GROUP
        hit = (ri == dks[0]) | (ri == dks[1]) | (ri == dks[2]) | (ri == dks[3])
        srt = jnp.dot(_mask_bf16(hit), h2, preferred_element_type=F32)
        buf[slot, rg * ROW_GROUP:(rg + 1) * ROW_GROUP, :] = srt.astype(ROW_DTYPE)

    last = LOCAL_ROWS // ROW_GROUP - 1
    for rg in range(last):
        sort_group(rg)

    @pl.when(_block_rows(tab_ref) > last * ROW_GROUP)
    def _():
        sort_group(last)

    _start_copies(tab_ref, cnt_s, slot, functools.partial(row_copy, slot))

    @pl.when(b == nb - 1)
    def _():
        _wait_copies(cnt_s, slot, functools.partial(row_copy, slot))

        @pl.when(b >= 1)
        def _():
            _wait_copies(cnt_s, 1 - slot, functools.partial(row_copy, 1 - slot))


def _dispatch(table, blk0, h2, gates, rtot, xs_prev=None):
    n, d = h2.shape
    nb = n // TOKEN_BLOCK
    in_specs = [pl.BlockSpec((1, 1, TABLE_W), lambda b: (b + blk0, 0, 0), memory_space=pltpu.SMEM),
                pl.BlockSpec((TOKEN_BLOCK, d), lambda b: (b, 0)),
                pl.BlockSpec((TOKEN_BLOCK, GATE_LANES), lambda b: (b, 0))]
    args = [table, h2, gates]
    aliases = {}
    if xs_prev is not None:
        in_specs.append(pl.BlockSpec(memory_space=pl.ANY))
        args.append(xs_prev)
        aliases = {3: 0}
    return pl.pallas_call(
        _dispatch_kernel,
        grid=(nb,),
        in_specs=in_specs,
        out_specs=pl.BlockSpec(memory_space=pl.ANY),
        out_shape=jax.ShapeDtypeStruct((rtot, d), ROW_DTYPE),
        scratch_shapes=[pltpu.VMEM((2, LOCAL_ROWS, d), ROW_DTYPE),
                        pltpu.SemaphoreType.DMA((2,)),
                        pltpu.SMEM((2 * len(COPY_CHUNKS),), jnp.int32)],
        input_output_aliases=aliases,
        compiler_params=_cparams("arbitrary"),
        name="moe_dispatch",
    )(*args)


def _combine_kernel(tab_ref, tabn_ref, x_ref, g2_ref, g_ref, y_hbm, o_ref, ybuf, sem, cnt_s):
    ch = CHUNK_ROWS
    tb = TOKEN_BLOCK
    gl = GATE_LANES
    b = pl.program_id(0)
    slot = b % 2
    n_rows = _block_rows(tab_ref)

    def row_copy(s, rows, local_row, global_row):
        return pltpu.make_async_copy(y_hbm.at[pl.ds(global_row, rows), :], ybuf.at[s, pl.ds(local_row, rows), :],
                                     sem.at[s])

    @pl.when(b == 0)
    def _():
        _start_copies(tab_ref, cnt_s, slot, functools.partial(row_copy, slot))

    _start_copies(tabn_ref, cnt_s, 1 - slot, functools.partial(row_copy, 1 - slot))

    g = g_ref[...]
    m = g > 0.0
    m_bf = _mask_bf16(m)
    ti = lax.broadcasted_iota(jnp.int32, (tb, tb), 0)
    tj = lax.broadcasted_iota(jnp.int32, (tb, tb), 1)
    rank = jnp.dot(_mask_bf16(tj < ti), m_bf, preferred_element_type=F32)
    cnt = jnp.sum(jnp.where(m, 1.0, 0.0), axis=0, keepdims=True)
    nch = jnp.floor((cnt + (ch - 1.0)) * (1.0 / ch))
    ei = lax.broadcasted_iota(jnp.int32, (gl, gl), 0)
    ej = lax.broadcasted_iota(jnp.int32, (gl, gl), 1)
    loc = ch * jnp.dot(jnp.broadcast_to(nch, (8, gl)).astype(BF16), _mask_bf16(ei < ej),
                       preferred_element_type=F32)[0:1, :]
    order = jnp.dot(m_bf, _mask_bf16(ei <= ej), preferred_element_type=F32)
    dest = loc + rank
    dks, gks = [], []
    for k in range(TOP_K):
        selk = m & (order == float(k + 1))
        dk = jnp.sum(jnp.where(selk, dest + 1.0, 0.0), axis=1, keepdims=True) - 1.0
        dks.append(dk.astype(jnp.int32))
        gks.append(jnp.sum(jnp.where(selk, g, 0.0), axis=1, keepdims=True))

    _wait_copies(cnt_s, slot, functools.partial(row_copy, slot))

    def fold_group(rg):
        li = lax.broadcasted_iota(jnp.int32, (tb, ROW_GROUP), 1) + rg * ROW_GROUP
        cmat = jnp.zeros((tb, ROW_GROUP), F32)
        for dk, gk in zip(dks, gks):
            cmat = cmat + jnp.where(li == dk, gk, 0.0)
        ri = lax.broadcasted_iota(jnp.int32, (ROW_GROUP, 1), 0) + rg * ROW_GROUP
        yv = ybuf[slot, rg * ROW_GROUP:(rg + 1) * ROW_GROUP, :]
        yv = jnp.where(ri < n_rows, yv, jnp.zeros_like(yv))
        return jnp.dot(cmat.astype(BF16), yv.astype(BF16), preferred_element_type=F32)

    last = LOCAL_ROWS // ROW_GROUP - 1
    acc = fold_group(0)
    for rg in range(1, last):
        acc = acc + fold_group(rg)
    o_ref[...] = x_ref[...] + g2_ref[0] * acc

    @pl.when(n_rows > last * ROW_GROUP)
    def _():
        o_ref[...] += g2_ref[0] * fold_group(last)


def _combine(table, blk0, x, g2, gates, ys):
    bn, sn, d = x.shape
    n = bn * sn
    per_seq = sn // TOKEN_BLOCK
    n_blocks = n // TOKEN_BLOCK
    empty_row = table.shape[0] - 1
    out = pl.pallas_call(
        _combine_kernel,
        grid=(n_blocks,),
        in_specs=[pl.BlockSpec((1, 1, TABLE_W), lambda b: (b + blk0, 0, 0), memory_space=pltpu.SMEM),
                  pl.BlockSpec((1, 1, TABLE_W),
                               lambda b: (jnp.where(b + 1 < n_blocks, b + blk0 + 1, empty_row), 0, 0),
                               memory_space=pltpu.SMEM),
                  pl.BlockSpec((TOKEN_BLOCK, d), lambda b: (b, 0)),
                  pl.BlockSpec((1, 1, d), lambda b: (b // per_seq, 0, 0)),
                  pl.BlockSpec((TOKEN_BLOCK, GATE_LANES), lambda b: (b, 0)),
                  pl.BlockSpec(memory_space=pl.ANY)],
        out_specs=pl.BlockSpec((TOKEN_BLOCK, d), lambda b: (b, 0)),
        out_shape=jax.ShapeDtypeStruct((n, d), F32),
        scratch_shapes=[pltpu.VMEM((2, LOCAL_ROWS, d), ROW_DTYPE),
                        pltpu.SemaphoreType.DMA((2,)),
                        pltpu.SMEM((2 * len(COPY_CHUNKS),), jnp.int32)],
        compiler_params=_cparams("arbitrary"),
        name="moe_combine",
    )(table, table, x.reshape(n, d), g2, gates.reshape(n, GATE_LANES), ys)
    return out.reshape(bn, sn, d)


TILE_ACTIVE, TILE_FIRST, TILE_SLOT, TILE_COLD = 1, 2, 4, 8
EXPERT_PARTS = 2


def _expert_kernel(te_ref, tf_ref, nv_ref, nx_ref, x_ref, wgu_hbm, bgu_ref, wdn_hbm, bdn_ref, y_ref,
                   wgu_st, wdn_st, wgu_s, wdn_s, sem, *, layer):
    i = pl.program_id(0)
    flag = tf_ref[i]
    d, f2 = wgu_s.shape
    f = f2 // 2
    rows = 128

    def weight_copies(e, s):
        return (pltpu.make_async_copy(wgu_hbm.at[layer, e], wgu_st.at[s], sem.at[0, s]),
                pltpu.make_async_copy(wdn_hbm.at[layer, e], wdn_st.at[s], sem.at[1, s]))

    @pl.when((flag & TILE_FIRST) != 0)
    def _():
        slot = (flag // TILE_SLOT) & 1

        @pl.when((flag & TILE_COLD) != 0)
        def _():
            for cp in weight_copies(te_ref[i], slot):
                cp.start()

        for cp in weight_copies(te_ref[i], slot):
            cp.wait()
        nxt = nx_ref[i]

        @pl.when(nxt >= 0)
        def _():
            for cp in weight_copies(nxt, 1 - slot):
                cp.start()

        def cast_gu(r, c):
            r0 = pl.multiple_of(r * rows, rows)
            wgu_s[pl.ds(r0, rows), :] = wgu_st[slot, pl.ds(r0, rows), :].astype(BF16)
            return c
        lax.fori_loop(0, d // rows, cast_gu, 0)

        def cast_dn(r, c):
            r0 = pl.multiple_of(r * rows, rows)
            wdn_s[pl.ds(r0, rows), :] = wdn_st[slot, pl.ds(r0, rows), :].astype(BF16)
            return c
        lax.fori_loop(0, f // rows, cast_dn, 0)

    @pl.when((flag & TILE_ACTIVE) != 0)
    def _():
        rows = x_ref.shape[0] // EXPERT_PARTS

        def up_project(p):
            ri = lax.broadcasted_iota(jnp.int32, (rows, 1), 0) + p * rows
            x = x_ref[p * rows:(p + 1) * rows, :]
            x = jnp.where(ri < nv_ref[i], x, jnp.zeros_like(x)).astype(BF16)
            return jnp.dot(x, wgu_s[...], preferred_element_type=F32) + bgu_ref[0]

        gus = [up_project(p) for p in range(EXPERT_PARTS)]
        for p, gu in enumerate(gus):
            gl = jnp.minimum(gu[:, :f], SWIGLU_LIMIT)
            up = jnp.clip(gu[:, f:], -SWIGLU_LIMIT, SWIGLU_LIMIT)
            hid = (up + 1.0) * gl * jax.nn.sigmoid(SWIGLU_ALPHA * gl)
            y = jnp.dot(hid.astype(BF16), wdn_s[...], preferred_element_type=F32) + bdn_ref[0]
            y_ref[p * rows:(p + 1) * rows, :] = y.astype(ROW_DTYPE)

    @pl.when((flag & TILE_ACTIVE) == 0)
    def _():
        y_ref[...] = jnp.zeros_like(y_ref)


def _experts(tile_expert, tile_flag, tile_rows, tile_next, xs, w_gu, b_gu, w_dn, b_dn, layer, tm):
    rtot, d = xs.shape
    _, ne, _, f2 = w_gu.shape
    f = f2 // 2
    grid_spec = pltpu.PrefetchScalarGridSpec(
        num_scalar_prefetch=4,
        grid=(rtot // tm,),
        in_specs=[pl.BlockSpec((tm, d), lambda i, te, tf, nv, nx: (i, 0)),
                  pl.BlockSpec(memory_space=pl.ANY),
                  pl.BlockSpec((1, 1, f2), lambda i, te, tf, nv, nx: (te[i], 0, 0)),
                  pl.BlockSpec(memory_space=pl.ANY),
                  pl.BlockSpec((1, 1, d), lambda i, te, tf, nv, nx: (te[i], 0, 0))],
        out_specs=pl.BlockSpec((tm, d), lambda i, te, tf, nv, nx: (i, 0)),
        scratch_shapes=[pltpu.VMEM((2, d, f2), F32), pltpu.VMEM((2, f, d), F32),
                        pltpu.VMEM((d, f2), BF16), pltpu.VMEM((f, d), BF16),
                        pltpu.SemaphoreType.DMA((2, 2))],
    )
    return pl.pallas_call(
        functools.partial(_expert_kernel, layer=layer),
        grid_spec=grid_spec,
        out_shape=jax.ShapeDtypeStruct((rtot, d), ROW_DTYPE),
        compiler_params=_cparams("arbitrary"),
        name="experts",
    )(tile_expert, tile_flag, tile_rows, tile_next, xs, w_gu, b_gu.reshape(ne, 1, f2), w_dn,
      b_dn.reshape(ne, 1, d))


def _moe_layout(gate_streams, tm):
    n = sum(gs.shape[0] for gs in gate_streams)
    nb = n // TOKEN_BLOCK
    ne = N_EXPERTS
    ch = CHUNK_ROWS
    tmc = tm // ch
    n_tiles = -(-(n * TOP_K + nb * ne * (ch - 1) + ne * (tm - ch)) // tm)
    i32 = jnp.int32
    cnt = jnp.concatenate(
        [jnp.sum((gs[:, :ne] > 0.0).astype(i32).reshape(-1, TOKEN_BLOCK, ne), axis=1) for gs in gate_streams], axis=0)
    nch = (cnt + ch - 1) // ch
    loc_end = jnp.cumsum(nch, axis=1)
    loc_off = loc_end - nch
    exp_ch = jnp.sum(nch, axis=0)
    exp_pad = ((exp_ch + tmc - 1) // tmc) * tmc
    exp_end = jnp.cumsum(exp_pad)
    exp_off = exp_end - exp_pad
    seg_off = exp_off[None, :] + jnp.cumsum(nch, axis=0) - nch
    per_seg = (nch // 4, (nch % 4) // 2, nch % 2)
    seg_start = (0 * nch, 4 * per_seg[0], 4 * per_seg[0] + 2 * per_seg[1])
    sections, totals = [], []
    for cnt_k, start_k, chunks, slots in zip(per_seg, seg_start, COPY_CHUNKS, COPY_SLOTS):
        cum = jnp.cumsum(cnt_k, axis=1)
        j = jnp.arange(slots, dtype=i32)
        e_of_j = jnp.minimum(jnp.sum((cum[:, None, :] <= j[None, :, None]).astype(i32), axis=2), ne - 1)
        hot = (e_of_j[:, :, None] == jnp.arange(ne, dtype=i32)).astype(i32)
        within = chunks * (j[None, :] - jnp.sum(hot * (cum - cnt_k)[:, None, :], axis=2))
        local = jnp.sum(hot * (loc_off + start_k)[:, None, :], axis=2) + within
        glob = jnp.sum(hot * (seg_off + start_k)[:, None, :], axis=2) + within
        sections.append(jnp.where(j[None, :] < cum[:, ne - 1:], glob * 256 + local, 0))
        totals.append(cum[:, ne - 1:])
    pad = jnp.zeros((nb, COUNT_BASE - sum(COPY_SLOTS)), i32)
    table = jnp.concatenate(sections + [pad] + totals, axis=1)
    table = jnp.concatenate([table, jnp.zeros((1, TABLE_W), i32)], axis=0)
    tile_start = jnp.arange(n_tiles, dtype=i32) * tmc
    te = jnp.minimum(jnp.sum((exp_end[None, :] <= tile_start[:, None]).astype(i32), axis=1), ne - 1)
    te_hot = (te[:, None] == jnp.arange(ne, dtype=i32)).astype(i32)
    t_off = jnp.sum(te_hot * exp_off[None, :], axis=1)
    t_len = jnp.sum(te_hot * exp_ch[None, :], axis=1)
    total = exp_end[ne - 1]
    active = tile_start < total
    first = active & (tile_start == t_off)
    tile_rows = jnp.clip((t_len - (tile_start - t_off)) * ch, 0, tm)
    last_active = (tile_start + tmc == total).astype(i32)
    te = jnp.where(active, te, jnp.sum(last_active * te))
    group = jnp.cumsum(first.astype(i32)) - 1
    ei = jnp.arange(ne, dtype=i32)
    later = (ei[None, :] > ei[:, None]) & (exp_ch[None, :] > 0)
    next_e = jnp.min(jnp.where(later, ei[None, :], ne), axis=1)
    next_e = jnp.where(next_e < ne, next_e, -1)
    tile_next = jnp.sum(te_hot * next_e[None, :], axis=1)
    tile_flag = (TILE_ACTIVE * active.astype(i32) + TILE_FIRST * first.astype(i32) + TILE_SLOT * (group & 1)
                 + TILE_COLD * (first & (group == 0)).astype(i32))
    return table.reshape(nb + 1, 1, TABLE_W), (te, tile_flag, tile_rows, tile_next), n_tiles * tm


def _rope_tables(sn, scale_q, roped):
    if not roped:
        ones = jnp.ones((sn, GROUP_W), F32)
        zeros = jnp.zeros((sn, GROUP_W), F32)
        return ones * scale_q, zeros, ones, zeros
    pos = jnp.arange(sn)
    row = (pos // GRID_W).astype(F32)
    col = (pos % GRID_W).astype(F32)
    nf = HEAD_QK_B // 4
    inv = ROPE_BASE ** (-jnp.arange(nf, dtype=F32) / nf)
    ar = row[:, None] * inv
    ac = col[:, None] * inv
    cos32 = jnp.concatenate([jnp.cos(ar), jnp.cos(ar), jnp.cos(ac), jnp.cos(ac)], axis=1)
    sin32 = jnp.concatenate([-jnp.sin(ar), jnp.sin(ar), -jnp.sin(ac), jnp.sin(ac)], axis=1)
    reps = GROUP_W // HEAD_QK_B
    cos = jnp.tile(cos32, (1, reps))
    sin = jnp.tile(sin32, (1, reps))
    return cos * scale_q, sin * scale_q, cos, sin


def kernel(x, c, ctx, c_ctx, w_mod, b_mod, norm1_g, norm2_g, w_in, a_vnorm_g, a_ws, a_bs, b_qnorm_g, b_knorm_g, b_lam_q1, b_lam_k1, b_lam_q2, b_lam_k2, b_subln_g, c_conv_w, c_conv_b, c_ln_g, c_ln_b, d_conv_w, w_out, router_w, router_b, exp_w_gu, exp_b_gu, exp_w_dn, exp_b_dn):
    bsz, seq, dm = x.shape
    ctx_len = ctx.shape[1]
    depth = w_in.shape[0]
    g = GROUP_W
    tm_lat = min(512, seq)
    tm_ctx = min(512, ctx_len)
    tq_lat = min(512, seq)
    tq_ctx = min(256, ctx_len)
    tm_moe = 512
    assert seq % TOKEN_BLOCK == 0 and ctx_len % TOKEN_BLOCK == 0

    n_mod_rows = -(-(bsz + 1) // 8) * 8
    cc = jnp.concatenate([c, c_ctx[None, :], jnp.zeros((n_mod_rows - bsz - 1, dm), F32)], axis=0)
    mods = _modulation(cc, w_mod, b_mod)

    gsum32 = _group_ones(g, HEAD_QK_B)
    gsum64 = _group_ones(g, HEAD_V_B)
    scale_q = HEAD_QK_B ** -0.5 * math.log2(math.e)
    tabs_lat = _rope_tables(seq, scale_q, True)
    tabs_ctx = _rope_tables(ctx_len, scale_q, False)

    xc = ctx
    for l in range(depth):
        last = l == depth - 1
        lam_init = 0.8 - 0.6 * math.exp(-0.3 * l)
        lam = (jnp.exp(jnp.sum(b_lam_q1[l] * b_lam_k1[l])) - jnp.exp(jnp.sum(b_lam_q2[l] * b_lam_k2[l]))
               + lam_init).astype(F32)
        score_bound = 1.01 * scale_q * HEAD_QK_B * jnp.max(jnp.abs(b_qnorm_g[l])) * jnp.max(jnp.abs(b_knorm_g[l]))
        lam = jnp.stack([lam, score_bound.astype(F32)])
        m_lat = mods[l, :bsz].reshape(bsz, 6, 1, dm)
        m_ctx = jnp.broadcast_to(mods[l, bsz].reshape(1, 6, 1, dm), (bsz, 6, 1, dm))
        sh1, sc1, g1, sh2, sc2, g2 = [m_lat[:, i] for i in range(6)]
        sh1c, sc1c, g1c, sh2c, sc2c, g2c = [m_ctx[:, i] for i in range(6)]

        w_in_bf = w_in[l].astype(BF16)
        n1g = norm1_g[l][None, :]
        n2g = norm2_g[l][None, :]
        reps = g // HEAD_QK_B // 2
        qg = jnp.tile(b_qnorm_g[l].reshape(1, 2 * HEAD_QK_B), (1, reps))
        kg = jnp.tile(b_knorm_g[l].reshape(1, 2 * HEAD_QK_B), (1, reps))
        gpost = jnp.tile(b_subln_g[l][None, :], (1, N_HEADS_B)) * (1.0 - lam_init)
        vg = a_vnorm_g[l][None, :]
        ws_bf = a_ws[l].astype(BF16)
        bs_exp = jnp.repeat(a_bs[l].T, HEAD_A, axis=1)
        mix_params = (vg, ws_bf, bs_exp, gsum64, c_conv_w[l], c_conv_b[l][None, :], c_ln_g[l][None, :],
                      c_ln_b[l][None, :], d_conv_w[l])
        wo_bf = w_out[l].astype(BF16)
        wr = jnp.pad(router_w[l], ((0, 0), (0, GATE_LANES - N_EXPERTS)))
        wr_hi = wr.astype(BF16)
        wr_lo = (wr - wr_hi.astype(F32)).astype(BF16)
        br = jnp.pad(router_b[l], (0, GATE_LANES - N_EXPERTS))[None, :]

        p_lat = _in_proj(x, sc1, sh1, n1g, w_in_bf, gsum32, qg, kg, tabs_lat, tm_lat)
        p_ctx = _in_proj(xc, sc1c, sh1c, n1g, w_in_bf, gsum32, qg, kg, tabs_ctx, tm_ctx)

        yb = _attention(lam, p_lat, (p_lat, p_ctx), gpost, gsum64, tq_lat)
        ya, yc, yd = _mixers(p_lat, *mix_params)
        x, h2, gates = _out_proj(ya, yb, yc, yd, x, g1, sc2, sh2, n2g, wo_bf, wr_hi, wr_lo, br, tm_lat)
        moe_w = (exp_w_gu, exp_b_gu[l], exp_w_dn, exp_b_dn[l], l, tm_moe)
        h2f = h2.reshape(-1, dm)
        gatesf = gates.reshape(-1, GATE_LANES)
        if last:
            table, tiles, rtot = _moe_layout((gatesf,), tm_moe)
            xs = _dispatch(table, 0, h2f, gatesf, rtot)
            ys = _experts(*tiles, xs, *moe_w)
            x = _combine(table, 0, x, g2, gates, ys)
        else:
            ybc = _attention(lam, p_ctx, (p_ctx,), gpost, gsum64, tq_ctx)
            yac, ycc, ydc = _mixers(p_ctx, *mix_params)
            xc, h2c, gatesc = _out_proj(yac, ybc, ycc, ydc, xc, g1c, sc2c, sh2c, n2g, wo_bf, wr_hi, wr_lo, br,
                                        tm_ctx)
            gatescf = gatesc.reshape(-1, GATE_LANES)
            blk_ctx = bsz * seq // TOKEN_BLOCK
            table, tiles, rtot = _moe_layout((gatesf, gatescf), tm_moe)
            xs = _dispatch(table, 0, h2f, gatesf, rtot)
            xs = _dispatch(table, blk_ctx, h2c.reshape(-1, dm), gatescf, rtot, xs_prev=xs)
            ys = _experts(*tiles, xs, *moe_w)
            x = _combine(table, 0, x, g2, gates, ys)
            xc = _combine(table, blk_ctx, xc, g2c, gatesc, ys)
    return x
```
